```python
import functools
import jax, jax.numpy as jnp
from jax import lax
import numpy as np

D_MODEL = 1024
BATCH = 8
SEQ = 4096
DEPTH = 1
DEC_BATCH = 128
DEC_SEQ = 8
PAST_LEN = 8192
PAGE_SIZE = 128

ATTN_WIDTH = D_MODEL // 2
HEAD_DIM = 64
N_HEADS = ATTN_WIDTH // HEAD_DIM
CONV_WIDTH = D_MODEL - ATTN_WIDTH
N_IDX_HEADS = 4
IDX_DIM = 64
TOPK_MAX = 256
TOPK_FRAC = 4
ROPE_DIM = HEAD_DIM // 4
ROPE_THETA = 500000.0
CONV_K = 3
D_FF = 2816
Q_BLOCK = 128
EPS = 1e-6

Q_END = ATTN_WIDTH
K_END = Q_END + ATTN_WIDTH
V_END = K_END + ATTN_WIDTH
QI_END = V_END + N_IDX_HEADS * IDX_DIM
KI_END = QI_END + IDX_DIM
WI_END = KI_END + N_IDX_HEADS
BG_END = WI_END + CONV_WIDTH
CG_END = BG_END + CONV_WIDTH
IN_WIDTH = CG_END + CONV_WIDTH
SPLIT_POINTS = (Q_END, K_END, V_END, QI_END, KI_END, WI_END, BG_END, CG_END)

kernel_name = "hymba_dsa_shortconv_convffn_adaln_step"


def rms_norm(x, g):
    x32 = x.astype(jnp.float32)
    y = x32 * lax.rsqrt(jnp.mean(x32 * x32, axis=-1, keepdims=True) + EPS)
    return y.astype(x.dtype) * g


def rope(x, pos):
    half = ROPE_DIM // 2
    inv = ROPE_THETA ** (-jnp.arange(0, ROPE_DIM, 2, dtype=jnp.float32) / ROPE_DIM)
    ang = pos.astype(jnp.float32)[:, None] * inv[None, :]
    cos = jnp.cos(ang)[:, None, :]
    sin = jnp.sin(ang)[:, None, :]
    xr = x[..., :ROPE_DIM].astype(jnp.float32)
    x1, x2 = xr[..., :half], xr[..., half:]
    rot = jnp.concatenate([x1 * cos - x2 * sin, x2 * cos + x1 * sin], axis=-1).astype(x.dtype)
    return jnp.concatenate([rot, x[..., ROPE_DIM:]], axis=-1)


def causal_dwconv(x, prev, w):
    T = x.shape[1]
    xp = jnp.concatenate([prev.astype(x.dtype), x], axis=1)
    y = w[0] * xp[:, 0:T]
    for j in range(1, CONV_K):
        y = y + w[j] * xp[:, j:j + T]
    return y, xp[:, T:]


def indexer_scores(qi, wi, ki):
    dots = jnp.einsum('bthd,bsd->bths', qi, ki, preferred_element_type=jnp.float32)
    dots = jax.nn.relu(dots * (IDX_DIM ** -0.5))
    return jnp.einsum('bths,bth->bts', dots, wi.astype(jnp.float32) * (N_IDX_HEADS ** -0.5))


def attend_selected(q, k_sel, v_sel, valid):
    logits = jnp.einsum('bthd,btkhd->bthk', q, k_sel, preferred_element_type=jnp.float32) * (HEAD_DIM ** -0.5)
    logits = jnp.where(valid[:, :, None, :], logits, -jnp.inf)
    p = jax.nn.softmax(logits, axis=-1)
    return jnp.einsum('bthk,btkhd->bthd', p.astype(v_sel.dtype), v_sel)


def prompt_sparse_attention(q, k, v, qi, wi, ki):
    B, S = q.shape[:2]
    n_blocks = S // Q_BLOCK
    top = min(TOPK_MAX, S // TOPK_FRAC)
    kpos = jnp.arange(S)
    bidx = jnp.arange(B)[:, None, None]

    def to_blocks(a):
        return jnp.moveaxis(a.reshape((B, n_blocks, Q_BLOCK) + a.shape[2:]), 1, 0)

    def one_block(args):
        qb, qib, wib, qpos = args
        sc = indexer_scores(qib, wib, ki)
        sc = jnp.where(kpos[None, None, :] <= qpos[None, :, None], sc, -jnp.inf)
        _, idx = lax.top_k(sc, top)
        valid = idx <= qpos[None, :, None]
        return attend_selected(qb, k[bidx, idx], v[bidx, idx], valid)

    out = lax.map(one_block, (to_blocks(q), to_blocks(qi), to_blocks(wi), kpos.reshape(n_blocks, Q_BLOCK)))
    return jnp.moveaxis(out, 0, 1).reshape(q.shape)


def sample_sparse_attention(q, k, v, qi, wi, ki, cache_k, cache_v, cache_kidx, page_table, layer):
    DB, T = q.shape[:2]
    past = page_table.shape[1] * PAGE_SIZE
    L = past + T
    top = min(TOPK_MAX, L // TOPK_FRAC)
    ki_past = cache_kidx[layer, page_table].reshape(DB, past, IDX_DIM)
    ki_all = jnp.concatenate([ki_past.astype(ki.dtype), ki], axis=1)
    qpos = past + jnp.arange(T)
    sc = indexer_scores(qi, wi, ki_all)
    sc = jnp.where(jnp.arange(L)[None, None, :] <= qpos[None, :, None], sc, -jnp.inf)
    _, idx = lax.top_k(sc, top)
    valid = idx <= qpos[None, :, None]
    is_new = (idx >= past)[..., None, None]
    bidx = jnp.arange(DB)[:, None, None]
    pidx = jnp.minimum(idx, past - 1)
    phys = page_table[bidx, pidx // PAGE_SIZE]
    off = pidx % PAGE_SIZE
    nidx = jnp.clip(idx - past, 0, T - 1)
    k_sel = jnp.where(is_new, k[bidx, nidx], cache_k[layer, phys, off].astype(k.dtype))
    v_sel = jnp.where(is_new, v[bidx, nidx], cache_v[layer, phys, off].astype(v.dtype))
    return attend_selected(q, k_sel, v_sel, valid)


def decoder_layer(x, c, pos, attn_fn, conv_prev, ffn_prev, w_ada, b_ada, g_mix_norm, w_in, g_q, g_k,
                  w_short_conv, w_o, g_ffn_norm, w_gate, w_up, w_ffn_conv, b_ffn_conv, w_down):
    B, T = x.shape[:2]
    mod = (jax.nn.silu(c) @ w_ada + b_ada)[:, None, :]
    sh1, sc1, gt1, sh2, sc2, gt2 = jnp.split(mod, 6, axis=-1)
    h = rms_norm(x, g_mix_norm) * (1 + sc1) + sh1
    proj = h @ w_in
    q, k, v, qi, ki, wi, b_gate, c_gate, u = jnp.split(proj, SPLIT_POINTS, axis=-1)
    q = rope(rms_norm(q.reshape(B, T, N_HEADS, HEAD_DIM), g_q), pos)
    k = rope(rms_norm(k.reshape(B, T, N_HEADS, HEAD_DIM), g_k), pos)
    v = v.reshape(B, T, N_HEADS, HEAD_DIM)
    qi = rope(qi.reshape(B, T, N_IDX_HEADS, IDX_DIM), pos)
    ki = rope(ki[:, :, None, :], pos)[:, :, 0, :]
    attn = attn_fn(q, k, v, qi, wi, ki).reshape(B, T, ATTN_WIDTH)
    conv_out, conv_state = causal_dwconv(c_gate * u, conv_prev, w_short_conv)
    mixed = jnp.concatenate([attn, b_gate * conv_out], axis=-1)
    x = x + gt1 * (mixed @ w_o)
    h2 = rms_norm(x, g_ffn_norm) * (1 + sc2) + sh2
    g_conv, ffn_state = causal_dwconv(h2 @ w_gate, ffn_prev, w_ffn_conv)
    x = x + gt2 * ((jax.nn.silu(g_conv + b_ffn_conv) * (h2 @ w_up)) @ w_down)
    return x, k, v, ki, conv_state, ffn_state


def setup_inputs(seed: int = 0) -> dict:
    key = jax.random.key(seed)
    ks = jax.random.split(key, 32)
    f32 = jnp.float32
    n_pages = PAST_LEN // PAGE_SIZE
    n_pool = (DEC_BATCH * n_pages * 5) // 4

    def nrm(k, shape, scale=1.0):
        return jax.random.normal(k, shape, f32) * scale

    page_table = jax.random.permutation(ks[7], n_pool)[:DEC_BATCH * n_pages].reshape(DEC_BATCH, n_pages).astype(jnp.int32)
    return {
        "x_prompt": nrm(ks[0], (BATCH, SEQ, D_MODEL)),
        "x_sample": nrm(ks[1], (DEC_BATCH, DEC_SEQ, D_MODEL)),
        "cache_k": nrm(ks[2], (DEPTH, n_pool, PAGE_SIZE, N_HEADS, HEAD_DIM)),
        "cache_v": nrm(ks[3], (DEPTH, n_pool, PAGE_SIZE, N_HEADS, HEAD_DIM)),
        "cache_kidx": nrm(ks[4], (DEPTH, n_pool, PAGE_SIZE, IDX_DIM)),
        "state_conv": nrm(ks[5], (DEPTH, DEC_BATCH, CONV_K - 1, CONV_WIDTH)),
        "state_ffn_conv": nrm(ks[6], (DEPTH, DEC_BATCH, CONV_K - 1, D_FF)),
        "page_table": page_table,
        "c_prompt": nrm(ks[8], (BATCH, D_MODEL)),
        "c_sample": nrm(ks[9], (DEC_BATCH, D_MODEL)),
        "w_ada": nrm(ks[10], (DEPTH, D_MODEL, 6 * D_MODEL), 0.5 * D_MODEL ** -0.5),
        "b_ada": nrm(ks[11], (DEPTH, 6 * D_MODEL), 0.01),
        "g_mix_norm": 1.0 + nrm(ks[12], (DEPTH, D_MODEL), 0.02),
        "w_in": nrm(ks[13], (DEPTH, D_MODEL, IN_WIDTH), D_MODEL ** -0.5),
        "g_q": 1.0 + nrm(ks[14], (DEPTH, HEAD_DIM), 0.02),
        "g_k": 1.0 + nrm(ks[15], (DEPTH, HEAD_DIM), 0.02),
        "w_short_conv": nrm(ks[16], (DEPTH, CONV_K, CONV_WIDTH), CONV_K ** -0.5),
        "w_o": nrm(ks[17], (DEPTH, D_MODEL, D_MODEL), D_MODEL ** -0.5),
        "g_ffn_norm": 1.0 + nrm(ks[18], (DEPTH, D_MODEL), 0.02),
        "w_gate": nrm(ks[19], (DEPTH, D_MODEL, D_FF), D_MODEL ** -0.5),
        "w_up": nrm(ks[20], (DEPTH, D_MODEL, D_FF), D_MODEL ** -0.5),
        "w_ffn_conv": nrm(ks[21], (DEPTH, CONV_K, D_FF), CONV_K ** -0.5),
        "b_ffn_conv": nrm(ks[22], (DEPTH, D_FF), 0.01),
        "w_down": nrm(ks[23], (DEPTH, D_FF, D_MODEL), D_FF ** -0.5),
    }


def reference(x_prompt, x_sample, cache_k, cache_v, cache_kidx, state_conv, state_ffn_conv, page_table,
              c_prompt, c_sample, w_ada, b_ada, g_mix_norm, w_in, g_q, g_k, w_short_conv, w_o,
              g_ffn_norm, w_gate, w_up, w_ffn_conv, b_ffn_conv, w_down):
    B, S = x_prompt.shape[:2]
    DB, T = x_sample.shape[:2]
    past = page_table.shape[1] * PAGE_SIZE
    pos_p = jnp.arange(S)
    pos_s = past + jnp.arange(T)
    xp, xs = x_prompt, x_sample
    kp_l, vp_l, kip_l, cp_l, fp_l = [], [], [], [], []
    ks_l, vs_l, kis_l, cs_l, fs_l = [], [], [], [], []
    for l in range(DEPTH):
        weights = (w_ada[l], b_ada[l], g_mix_norm[l], w_in[l], g_q[l], g_k[l], w_short_conv[l], w_o[l],
                   g_ffn_norm[l], w_gate[l], w_up[l], w_ffn_conv[l], b_ffn_conv[l], w_down[l])
        zeros_conv = jnp.zeros((B, CONV_K - 1, CONV_WIDTH), xp.dtype)
        zeros_ffn = jnp.zeros((B, CONV_K - 1, D_FF), xp.dtype)
        xp, kp, vp, kip, cp, fp = decoder_layer(xp, c_prompt, pos_p, prompt_sparse_attention,
                                                zeros_conv, zeros_ffn, *weights)
        attn_s = functools.partial(sample_sparse_attention, cache_k=cache_k, cache_v=cache_v,
                                   cache_kidx=cache_kidx, page_table=page_table, layer=l)
        xs, ks, vs, kis, cs, fs = decoder_layer(xs, c_sample, pos_s, attn_s,
                                                state_conv[l], state_ffn_conv[l], *weights)
        kp_l.append(kp); vp_l.append(vp); kip_l.append(kip); cp_l.append(cp); fp_l.append(fp)
        ks_l.append(ks); vs_l.append(vs); kis_l.append(kis); cs_l.append(cs); fs_l.append(fs)
    return (xp, xs,
            jnp.stack(kp_l), jnp.stack(vp_l), jnp.stack(kip_l), jnp.stack(cp_l), jnp.stack(fp_l),
            jnp.stack(ks_l), jnp.stack(vs_l), jnp.stack(kis_l), jnp.stack(cs_l), jnp.stack(fs_l))
```

```python
import functools

import jax
import jax.numpy as jnp
from jax import lax
from jax.experimental import pallas as pl
from jax.experimental.pallas import tpu as pltpu

F32 = jnp.float32
BF16 = jnp.bfloat16
I32 = jnp.int32

HEAD_DIM = 64
IDX_DIM = 64
N_IDX_HEADS = 4
ROPE_DIM = 16
ROPE_THETA = 500000.0
CONV_K = 3
TOPK_MAX = 256
TOPK_FRAC = 4
PAGE_SIZE = 128
EPS = 1e-6

LANES = 128
IDX_PAD = 384
QI3 = 256
NEG_BIAS = -1e30
INT_MIN = -(2 ** 31)
VMEM_LIMIT = 56 * 1024 * 1024

_NT = (((1,), (1,)), ((), ()))


def _sortable(x):
    b = lax.bitcast_convert_type(x, I32)
    return b ^ (lax.shift_right_arithmetic(b, 31) & 0x7FFFFFFF)


_NEG_INF_KEY = (0xFF800000 ^ 0x7FFFFFFF) - 2 ** 32


def _lane_low(shape):
    return (lax.broadcasted_iota(I32, shape, len(shape) - 1) % LANES) < (LANES // 2)


def _split_bf16(x):
    hi = x.astype(BF16)
    lo = (x - hi.astype(F32)).astype(BF16)
    return hi, lo


def _select_params(count, top, nbits_idx, ones):
    kf = float(top)
    zero = ones * 0
    one_if = lambda cond: jnp.where(cond, 1.0, 0.0)
    c0 = count(lambda kk, idx: one_if(kk >= zero))
    t0 = jnp.where(c0 >= kf, zero, zero + INT_MIN)

    def tstep(i, t):
        cand = t | jnp.left_shift(ones, 30 - i)
        c = count(lambda kk, idx: one_if(kk >= cand))
        return jnp.where(c >= kf, cand, t)

    t = lax.fori_loop(0, 31, tstep, t0)
    cgt = count(lambda kk, idx: one_if(kk > t))
    cge = count(lambda kk, idx: one_if(kk >= t))
    r = kf - cgt

    def jsearch():
        def jstep(i, j):
            cand = j | jnp.left_shift(ones, nbits_idx - 1 - i)
            c = count(lambda kk, idx: jnp.where(kk == t, one_if(idx < cand), 0.0))
            return jnp.where(c < r, cand, j)

        return lax.fori_loop(0, nbits_idx, jstep, zero)

    need = jnp.max(cge) > kf
    j = lax.cond(need, jsearch, lambda: zero + (2 ** nbits_idx))
    return t, j


def _bias_tile(kk, idx, t, j):
    tie = jnp.where(kk == t, jnp.where(idx <= j, 0.0, NEG_BIAS), NEG_BIAS)
    bias = jnp.where(kk > t, 0.0, tie)
    return jnp.where(kk == _NEG_INF_KEY, NEG_BIAS, bias)


def _ada_kernel(c_ref, w_ref, b_ref, o_ref):
    c = c_ref[...]
    a = c * jax.nn.sigmoid(c)
    o_ref[...] = jnp.dot(a, w_ref[...], preferred_element_type=F32,
                         precision=lax.Precision.HIGHEST) + b_ref[...]


def _ada(c_all, w_ada, b_ada):
    n, d = c_all.shape
    n6 = w_ada.shape[1]
    bn = d
    return pl.pallas_call(
        _ada_kernel,
        grid=(n6 // bn,),
        in_specs=[pl.BlockSpec((n, d), lambda j: (0, 0)),
                  pl.BlockSpec((d, bn), lambda j: (0, j)),
                  pl.BlockSpec((1, bn), lambda j: (0, j))],
        out_specs=pl.BlockSpec((n, bn), lambda j: (0, j)),
        out_shape=jax.ShapeDtypeStruct((n, n6), F32),
        compiler_params=pltpu.CompilerParams(dimension_semantics=("arbitrary",),
                                             vmem_limit_bytes=VMEM_LIMIT),
        name="ada",
    )(c_all, w_ada, b_ada.reshape(1, n6))


def _rope(x2, nseq, rows, cos, sa, sb):
    m, w = x2.shape
    xm = pltpu.roll(x2, w - ROPE_DIM // 2, 1)
    xp = pltpu.roll(x2, ROPE_DIM // 2, 1)
    y = (x2.reshape(nseq, rows, w) * cos[None] + xm.reshape(nseq, rows, w) * sa[None]
         + xp.reshape(nseq, rows, w) * sb[None])
    return y.reshape(m, w)


def _inproj_kernel(x_ref, mod_ref, gmix_ref, wqkv_ref, wih_ref, wil_ref, wcv_ref, bd_ref, gq_ref, gk_ref,
                   cos_ref, sa_ref, sb_ref, wsc_ref, prev_ref,
                   qexp_ref, k_ref, v_ref, kb_ref, vb_ref, qi3_ref, ki3_ref, wi_ref, kidx_ref, bconv_ref,
                   cst_ref, cbuf_ref, *, carry):
    nseq, rows, d = x_ref.shape
    m = nseq * rows
    aw = k_ref.shape[-1]
    cw = bconv_ref.shape[-1]
    nh = aw // HEAD_DIM

    x = x_ref[...]
    mod = mod_ref[...]
    sh1 = mod[:, :, 0:d]
    sc1 = mod[:, :, d:2 * d]
    ms = jnp.mean(x * x, axis=-1, keepdims=True)
    h = (x * lax.rsqrt(ms + EPS)) * gmix_ref[...][None]
    h = h * (1.0 + sc1) + sh1
    h2 = h.reshape(m, d)
    hb, hl = _split_bf16(h2)

    cos = cos_ref[...]
    sa = sa_ref[...]
    sb = sb_ref[...]
    reps = aw // LANES
    cos_w = jnp.concatenate([cos] * reps, axis=1)
    sa_w = jnp.concatenate([sa] * reps, axis=1)
    sb_w = jnp.concatenate([sb] * reps, axis=1)

    qkv = jnp.dot(hb, wqkv_ref[...], preferred_element_type=F32)
    q = qkv[:, 0:aw]
    k = qkv[:, aw:2 * aw]
    v = qkv[:, 2 * aw:3 * aw]

    def head_norm(t, g):
        ss = jnp.dot((t * t).astype(BF16), bd_ref[...], preferred_element_type=F32)
        return (t * lax.rsqrt(ss + EPS)) * g

    q = _rope(head_norm(q, gq_ref[...]), nseq, rows, cos_w, sa_w, sb_w)
    k = _rope(head_norm(k, gk_ref[...]), nseq, rows, cos_w, sa_w, sb_w)
    k_ref[...] = k.reshape(nseq, rows, aw)
    v_ref[...] = v.reshape(nseq, rows, aw)
    kb_ref[...] = k.reshape(nseq, rows, aw).astype(kb_ref.dtype)
    vb_ref[...] = v.reshape(nseq, rows, aw).astype(vb_ref.dtype)

    low = _lane_low((1, LANES))
    qs = q * (HEAD_DIM ** -0.5)
    pieces = []
    for hh in range(nh):
        chunk = qs[:, (hh // 2) * LANES:(hh // 2 + 1) * LANES]
        keep = low if hh % 2 == 0 else jnp.logical_not(low)
        pieces.append(jnp.where(keep, chunk, 0.0))
    qexp_ref[...] = jnp.concatenate(pieces, axis=1).reshape(nseq, rows, nh * LANES).astype(qexp_ref.dtype)

    pidx = (jnp.dot(hb, wih_ref[...], preferred_element_type=F32)
            + jnp.dot(hl, wih_ref[...], preferred_element_type=F32)
            + jnp.dot(hb, wil_ref[...], preferred_element_type=F32))
    cos_k = jnp.where(low, cos, 1.0)
    sa_k = jnp.where(low, sa, 0.0)
    sb_k = jnp.where(low, sb, 0.0)
    pidx = _rope(pidx, nseq, rows,
                 jnp.concatenate([cos, cos, cos_k], axis=1),
                 jnp.concatenate([sa, sa, sa_k], axis=1),
                 jnp.concatenate([sb, sb, sb_k], axis=1))
    nqi = N_IDX_HEADS * IDX_DIM
    kt = pidx[:, nqi:nqi + LANES]
    kidx_ref[...] = kt[:, 0:IDX_DIM].reshape(nseq, rows, IDX_DIM)
    wi_ref[...] = (kt * ((IDX_DIM ** -0.5) * (N_IDX_HEADS ** -0.5))).reshape(nseq, rows, LANES)

    zero = jnp.zeros((m, LANES), F32)
    q3 = []
    for t in range(nqi // LANES):
        xt = pidx[:, t * LANES:(t + 1) * LANES]
        hi = xt.astype(BF16).astype(F32)
        lo = xt - hi
        hi_r = pltpu.roll(hi, LANES // 2, 1)
        lo_r = pltpu.roll(lo, LANES // 2, 1)
        q3 += [jnp.where(low, hi, hi_r), jnp.where(low, lo, zero),
               jnp.where(low, hi_r, hi), jnp.where(low, lo_r, zero)]
    qi3_ref[...] = jnp.concatenate(q3, axis=1).reshape(nseq, rows, N_IDX_HEADS * QI3).astype(qi3_ref.dtype)
    khi = kt.astype(BF16).astype(F32)
    klo = kt - khi
    k3 = jnp.concatenate([jnp.where(low, khi, pltpu.roll(klo, LANES // 2, 1)), jnp.where(low, khi, zero)], axis=1)
    ki3_ref[...] = k3.reshape(nseq, rows, QI3).astype(ki3_ref.dtype)

    pcv = jnp.dot(hb, wcv_ref[...], preferred_element_type=F32)
    bg = pcv[:, 0:cw].reshape(nseq, rows, cw)
    cu = (pcv[:, cw:2 * cw] * pcv[:, 2 * cw:3 * cw]).reshape(nseq, rows, cw)
    if carry:
        i = pl.program_id(1)

        @pl.when(i == 0)
        def _():
            cbuf_ref[:, 6:8, :] = jnp.zeros((nseq, 2, cw), F32)

        @pl.when(i > 0)
        def _():
            cbuf_ref[:, 6:8, :] = cbuf_ref[:, rows + 6:rows + 8, :]
    else:
        cbuf_ref[:, 6:8, :] = prev_ref[...]
    cbuf_ref[:, 8:, :] = cu
    wsc = wsc_ref[...]
    conv = wsc[0:1, :][None] * cbuf_ref[:, 6:6 + rows, :]
    conv = conv + wsc[1:2, :][None] * cbuf_ref[:, 7:7 + rows, :]
    conv = conv + wsc[2:3, :][None] * cu
    bconv_ref[...] = (bg * conv).astype(bconv_ref.dtype)
    cst_ref[...] = cu[:, rows - 2:rows, :]


def _const_spec(shape):
    nd = len(shape)
    return pl.BlockSpec(shape, lambda *_: (0,) * nd, pipeline_mode=pl.Buffered(1))


def _inproj(x, mod, prev, wts, tabs, *, carry, rows, nseq, act_dtype):
    nb, t, d = x.shape
    aw = d // 2
    cw = d - aw
    nh = aw // HEAD_DIM
    if carry:
        grid = (nb, t // rows)
        tile = lambda c: pl.BlockSpec((1, rows, c), lambda b, i: (b, i, 0))
        per_seq = lambda r, c: pl.BlockSpec((1, r, c), lambda b, i: (b, 0, 0))
        tab = pl.BlockSpec((rows, LANES), lambda b, i: (i, 0))
    else:
        grid = (1, nb // nseq)
        tile = lambda c: pl.BlockSpec((nseq, rows, c), lambda b, i: (i, 0, 0))
        per_seq = lambda r, c: pl.BlockSpec((nseq, r, c), lambda b, i: (i, 0, 0))
        tab = pl.BlockSpec((rows, LANES), lambda b, i: (0, 0))
    cos, sa, sb = tabs
    in_specs = [tile(d), per_seq(1, mod.shape[-1]), _const_spec((1, d)),
                _const_spec(wts["wqkv"].shape), _const_spec(wts["wih"].shape), _const_spec(wts["wil"].shape),
                _const_spec(wts["wcv"].shape), _const_spec(wts["bd"].shape),
                _const_spec((1, aw)), _const_spec((1, aw)), tab, tab, tab,
                _const_spec((CONV_K, cw)), per_seq(CONV_K - 1, cw)]
    outs = [("qexp", nh * LANES, act_dtype), ("k", aw, F32), ("v", aw, F32), ("kb", aw, act_dtype),
            ("vb", aw, act_dtype), ("qi3", N_IDX_HEADS * QI3, act_dtype), ("ki3", QI3, act_dtype),
            ("wi", LANES, F32), ("kidx", IDX_DIM, F32), ("bconv", cw, act_dtype)]
    out_specs = [tile(c) for _, c, _ in outs] + [per_seq(CONV_K - 1, cw)]
    out_shape = [jax.ShapeDtypeStruct((nb, t, c), dt) for _, c, dt in outs]
    out_shape.append(jax.ShapeDtypeStruct((nb, CONV_K - 1, cw), F32))
    res = pl.pallas_call(
        functools.partial(_inproj_kernel, carry=carry),
        grid=grid, in_specs=in_specs, out_specs=out_specs, out_shape=out_shape,
        scratch_shapes=[pltpu.VMEM((nseq, rows + 8, cw), F32)],
        compiler_params=pltpu.CompilerParams(dimension_semantics=("arbitrary", "arbitrary"),
                                             vmem_limit_bytes=VMEM_LIMIT),
        name="inproj_prompt" if carry else "inproj_sample",
    )(x, mod, wts["gmix"], wts["wqkv"], wts["wih"], wts["wil"], wts["wcv"], wts["bd"], wts["gq"], wts["gk"],
      cos, sa, sb, wts["wsc"], prev)
    named = {name: r for (name, _, _), r in zip(outs, res)}
    named["cstate"] = res[-1]
    return named


def _pattn_kernel(qexp_ref, qi3_ref, wi_ref, ki3_ref, kb_ref, vb_ref, o_ref, keys_ref, m_ref, l_ref, acc_ref,
                  *, top, blk, seq):
    nh = acc_ref.shape[0]
    nlt = blk // LANES
    qb = pl.program_id(1)
    nkb = qb + 1
    q0 = qb * blk
    wi = wi_ref[0]
    row = lax.broadcasted_iota(I32, (blk, blk), 0)
    col = lax.broadcasted_iota(I32, (blk, blk), 1)
    lane = lax.broadcasted_iota(I32, (blk, LANES), 1)
    ones = jnp.ones((blk, LANES), I32)

    def score_body(kb, carry_):
        off = pl.multiple_of(kb * blk, blk)
        kblk = ki3_ref[0, pl.ds(off, blk), :]
        acc = jnp.zeros((blk, blk), F32)
        for hh in range(N_IDX_HEADS):
            dd = lax.dot_general(qi3_ref[0, :, hh * QI3:(hh + 1) * QI3], kblk, _NT, preferred_element_type=F32)
            acc = acc + jnp.maximum(dd, 0.0) * wi[:, IDX_DIM + hh:IDX_DIM + hh + 1]
        acc = jnp.where(acc == 0.0, 0.0, acc)
        acc = jnp.where(kb * blk + col <= q0 + row, acc, -jnp.inf)
        keys_ref[kb] = _sortable(acc)
        return carry_

    lax.fori_loop(0, nkb, score_body, 0)

    def count(pred):
        def body(kb, c):
            for jj in range(nlt):
                kk = keys_ref[kb, :, jj * LANES:(jj + 1) * LANES]
                idx = kb * blk + jj * LANES + lane
                c = c + pred(kk, idx)
            return c

        c = lax.fori_loop(0, nkb, body, jnp.zeros((blk, LANES), F32))
        return jnp.sum(c, axis=1, keepdims=True)

    t, j = _select_params(count, top, (seq - 1).bit_length(), ones)

    m_ref[...] = jnp.full(m_ref.shape, -jnp.inf, F32)
    l_ref[...] = jnp.zeros(l_ref.shape, F32)
    acc_ref[...] = jnp.zeros(acc_ref.shape, F32)

    def attn_body(kb, carry_):
        off = pl.multiple_of(kb * blk, blk)
        bias = jnp.concatenate(
            [_bias_tile(keys_ref[kb, :, jj * LANES:(jj + 1) * LANES], kb * blk + jj * LANES + lane, t, j)
             for jj in range(nlt)], axis=1)
        for hh in range(nh):
            pp = hh // 2
            qh = qexp_ref[0, :, hh * LANES:(hh + 1) * LANES]
            kh = kb_ref[0, pl.ds(off, blk), pp * LANES:(pp + 1) * LANES]
            vh = vb_ref[0, pl.ds(off, blk), pp * LANES:(pp + 1) * LANES]
            s = lax.dot_general(qh, kh, _NT, preferred_element_type=F32) + bias
            m_prev = m_ref[hh]
            m_new = jnp.maximum(m_prev, jnp.max(s, axis=1, keepdims=True))
            alpha = jnp.exp(m_prev - m_new)
            p = jnp.exp(s - jnp.concatenate([m_new] * nlt, axis=1))
            l_ref[hh] = alpha * l_ref[hh] + jnp.sum(p, axis=1, keepdims=True)
            acc_ref[hh] = alpha * acc_ref[hh] + jnp.dot(p.astype(BF16), vh, preferred_element_type=F32)
            m_ref[hh] = m_new
        return carry_

    lax.fori_loop(0, nkb, attn_body, 0)

    low = _lane_low((1, LANES))
    outs = []
    for pp in range(nh // 2):
        a0 = acc_ref[2 * pp] / l_ref[2 * pp]
        a1 = acc_ref[2 * pp + 1] / l_ref[2 * pp + 1]
        outs.append(jnp.where(low, a0, a1))
    o_ref[0] = jnp.concatenate(outs, axis=1).astype(o_ref.dtype)


def _pattn(p, *, top, blk):
    nb, seq, aw = p["kb"].shape
    nh = aw // HEAD_DIM
    qtile = lambda c: pl.BlockSpec((1, blk, c), lambda b, i: (b, i, 0))
    whole = lambda c: pl.BlockSpec((1, seq, c), lambda b, i: (b, 0, 0))
    return pl.pallas_call(
        functools.partial(_pattn_kernel, top=top, blk=blk, seq=seq),
        grid=(nb, seq // blk),
        in_specs=[qtile(nh * LANES), qtile(N_IDX_HEADS * QI3), qtile(LANES), whole(QI3), whole(aw), whole(aw)],
        out_specs=qtile(aw),
        out_shape=jax.ShapeDtypeStruct((nb, seq, aw), BF16),
        scratch_shapes=[pltpu.VMEM((seq // blk, blk, blk), I32),
                        pltpu.VMEM((nh, blk, LANES), F32),
                        pltpu.VMEM((nh, blk, LANES), F32),
                        pltpu.VMEM((nh, blk, LANES), F32)],
        compiler_params=pltpu.CompilerParams(dimension_semantics=("arbitrary", "arbitrary"),
                                             vmem_limit_bytes=VMEM_LIMIT),
        name="pattn",
    )(p["qexp"], p["qi3"], p["wi"], p["ki3"], p["kb"], p["vb"])


def _sidx_kernel(pt_ref, qi3_ref, wi_ref, kin_ref, *rest, top, ppc, npages, nchunks):
    pages = rest[:ppc]
    bias_ref = rest[ppc]
    keys_ref = rest[ppc + 1]
    t_new = qi3_ref.shape[1]
    c = pl.program_id(1)
    qi3 = qi3_ref[0]
    wi = wi_ref[0]
    qh = jnp.concatenate([qi3[:, hh * QI3:hh * QI3 + IDX_DIM] for hh in range(N_IDX_HEADS)], axis=0).astype(BF16)
    ql = jnp.concatenate([qi3[:, hh * QI3 + LANES:hh * QI3 + LANES + IDX_DIM] for hh in range(N_IDX_HEADS)],
                         axis=0).astype(BF16)

    def head_sum(dd):
        acc = jnp.zeros((t_new, LANES), F32)
        for hh in range(N_IDX_HEADS):
            acc = acc + jnp.maximum(dd[hh * t_new:(hh + 1) * t_new], 0.0) * wi[:, IDX_DIM + hh:IDX_DIM + hh + 1]
        return jnp.where(acc == 0.0, 0.0, acc)

    for i in range(ppc):
        kp = pages[i][0]
        kh, kl = _split_bf16(kp)
        dd = (lax.dot_general(qh, kh, _NT, preferred_element_type=F32)
              + lax.dot_general(qh, kl, _NT, preferred_element_type=F32)
              + lax.dot_general(ql, kh, _NT, preferred_element_type=F32))
        keys_ref[c * ppc + i] = _sortable(head_sum(dd))

    @pl.when(c == nchunks - 1)
    def _():
        q3 = jnp.concatenate([qi3[:, hh * QI3:(hh + 1) * QI3] for hh in range(N_IDX_HEADS)], axis=0).astype(BF16)
        kn = jnp.concatenate([kin_ref[0], jnp.zeros((LANES - t_new, QI3), F32)], axis=0).astype(BF16)
        sc = head_sum(lax.dot_general(q3, kn, _NT, preferred_element_type=F32))
        row = lax.broadcasted_iota(I32, (t_new, LANES), 0)
        lane = lax.broadcasted_iota(I32, (t_new, LANES), 1)
        sc = jnp.where(lane <= row, sc, -jnp.inf)
        keys_ref[npages] = _sortable(sc)
        ones = jnp.ones((t_new, LANES), I32)

        def count(pred):
            def body(pg, cc):
                return cc + pred(keys_ref[pg], pg * LANES + lane)

            cc = lax.fori_loop(0, npages + 1, body, jnp.zeros((t_new, LANES), F32))
            return jnp.sum(cc, axis=1, keepdims=True)

        t, j = _select_params(count, top, ((npages + 1) * LANES - 1).bit_length(), ones)
        for pg in range(npages + 1):
            bias_ref[0, :, pg * LANES:(pg + 1) * LANES] = _bias_tile(keys_ref[pg], pg * LANES + lane, t, j)


def _sidx(s, cache_kidx, page_table, *, top, ppc):
    nb, t_new, _ = s["qi3"].shape
    npages = page_table.shape[1]
    nchunks = npages // ppc
    lp = (npages + 1) * LANES
    seq_blk = lambda c: pl.BlockSpec((1, t_new, c), lambda b, ch, pt: (b, 0, 0))

    def page_spec(i):
        return pl.BlockSpec((1, PAGE_SIZE, IDX_DIM), lambda b, ch, pt: (pt[b, ch * ppc + i], 0, 0))

    grid_spec = pltpu.PrefetchScalarGridSpec(
        num_scalar_prefetch=1, grid=(nb, nchunks),
        in_specs=[seq_blk(N_IDX_HEADS * QI3), seq_blk(LANES), seq_blk(QI3)] + [page_spec(i) for i in range(ppc)],
        out_specs=pl.BlockSpec((1, t_new, lp), lambda b, ch, pt: (b, 0, 0)),
        scratch_shapes=[pltpu.VMEM((npages + 1, t_new, LANES), I32)])
    return pl.pallas_call(
        functools.partial(_sidx_kernel, top=top, ppc=ppc, npages=npages, nchunks=nchunks),
        grid_spec=grid_spec,
        out_shape=jax.ShapeDtypeStruct((nb, t_new, lp), F32),
        compiler_params=pltpu.CompilerParams(dimension_semantics=("arbitrary", "arbitrary"),
                                             vmem_limit_bytes=VMEM_LIMIT),
        name="sidx",
    )(page_table, s["qi3"], s["wi"], s["ki3"], *([cache_kidx] * ppc))


def _sattn_kernel(pt_ref, qexp_ref, bias_ref, biasn_ref, kn_ref, vn_ref, *rest, ppc, nchunks):
    kpages = rest[:ppc]
    vpages = rest[ppc:2 * ppc]
    o_ref, qbd_ref, m_ref, l_ref, acc_ref = rest[2 * ppc:]
    t_new = kn_ref.shape[1]
    aw = kn_ref.shape[2]
    nh = aw // HEAD_DIM
    npair = nh // 2
    c = pl.program_id(1)

    @pl.when(c == 0)
    def _():
        qe = qexp_ref[0]
        zero = jnp.zeros((t_new, LANES), F32)
        rows_ = []
        for hh in range(nh):
            chunk = qe[:, hh * LANES:(hh + 1) * LANES]
            rows_.append(jnp.concatenate([chunk if tt == hh // 2 else zero for tt in range(npair)], axis=1))
        qbd_ref[...] = jnp.concatenate(rows_, axis=0).astype(BF16)
        m_ref[...] = jnp.full(m_ref.shape, -jnp.inf, F32)
        l_ref[...] = jnp.zeros(l_ref.shape, F32)
        acc_ref[...] = jnp.zeros(acc_ref.shape, F32)

    def step(kb, vb, bias_t):
        s = lax.dot_general(qbd_ref[...], kb, _NT, preferred_element_type=F32)
        s = s + jnp.concatenate([bias_t] * nh, axis=0)
        m_prev = m_ref[...]
        m_new = jnp.maximum(m_prev, jnp.max(s, axis=1, keepdims=True))
        alpha = jnp.exp(m_prev - m_new)
        p = jnp.exp(s - m_new)
        l_ref[...] = alpha * l_ref[...] + jnp.sum(p, axis=1, keepdims=True)
        acc_ref[...] = (jnp.concatenate([alpha] * (aw // LANES), axis=1) * acc_ref[...]
                        + jnp.dot(p.astype(BF16), vb, preferred_element_type=F32))
        m_ref[...] = m_new

    for i in range(ppc):
        step(kpages[i][0].astype(BF16), vpages[i][0].astype(BF16), bias_ref[0, :, i * LANES:(i + 1) * LANES])

    @pl.when(c == nchunks - 1)
    def _():
        pad = jnp.zeros((LANES - t_new, aw), F32)
        step(jnp.concatenate([kn_ref[0], pad], axis=0).astype(BF16),
             jnp.concatenate([vn_ref[0], pad], axis=0).astype(BF16), biasn_ref[0])
        low = _lane_low((1, LANES))
        acc = acc_ref[...]
        inv = 1.0 / l_ref[...]
        outs = []
        for pp in range(npair):
            r0 = slice(2 * pp * t_new, (2 * pp + 1) * t_new)
            r1 = slice((2 * pp + 1) * t_new, (2 * pp + 2) * t_new)
            cs = slice(pp * LANES, (pp + 1) * LANES)
            outs.append(jnp.where(low, acc[r0, cs] * inv[r0], acc[r1, cs] * inv[r1]))
        o_ref[0] = jnp.concatenate(outs, axis=1)


def _sattn(s, bias, cache_k, cache_v, page_table, *, ppc):
    nb, t_new, aw = s["k"].shape
    nh = aw // HEAD_DIM
    npages = page_table.shape[1]
    nchunks = npages // ppc
    seq_blk = lambda c: pl.BlockSpec((1, t_new, c), lambda b, ch, pt: (b, 0, 0))

    def page_spec(i):
        return pl.BlockSpec((1, PAGE_SIZE, aw), lambda b, ch, pt: (pt[b, ch * ppc + i], 0, 0))

    grid_spec = pltpu.PrefetchScalarGridSpec(
        num_scalar_prefetch=1, grid=(nb, nchunks),
        in_specs=[seq_blk(nh * LANES),
                  pl.BlockSpec((1, t_new, ppc * LANES), lambda b, ch, pt: (b, 0, ch)),
                  pl.BlockSpec((1, t_new, LANES), lambda b, ch, pt: (b, 0, npages)),
                  seq_blk(aw), seq_blk(aw)] + [page_spec(i) for i in range(ppc)] * 2,
        out_specs=seq_blk(aw),
        scratch_shapes=[pltpu.VMEM((nh * t_new, aw), BF16),
                        pltpu.VMEM((nh * t_new, LANES), F32),
                        pltpu.VMEM((nh * t_new, LANES), F32),
                        pltpu.VMEM((nh * t_new, aw), F32)])
    return pl.pallas_call(
        functools.partial(_sattn_kernel, ppc=ppc, nchunks=nchunks),
        grid_spec=grid_spec,
        out_shape=jax.ShapeDtypeStruct((nb, t_new, aw), F32),
        compiler_params=pltpu.CompilerParams(dimension_semantics=("arbitrary", "arbitrary"),
                                             vmem_limit_bytes=VMEM_LIMIT),
        name="sattn",
    )(page_table, s["qexp"], bias, bias, s["k"], s["v"], *([cache_k] * ppc), *([cache_v] * ppc))


def _ffn_kernel(x_ref, mod_ref, attn_ref, bconv_ref, wot_ref, wob_ref, g2_ref, wg_ref, wu_ref, wfc_ref, bfc_ref,
                wd_ref, prev_ref, y_ref, fst_ref, gbuf_ref, *, carry, fchunk):
    nseq, rows, d = x_ref.shape
    m = nseq * rows
    f = wg_ref.shape[1]
    x = x_ref[...]
    mod = mod_ref[...]
    gt1 = mod[:, :, 2 * d:3 * d]
    sh2 = mod[:, :, 3 * d:4 * d]
    sc2 = mod[:, :, 4 * d:5 * d]
    gt2 = mod[:, :, 5 * d:6 * d]
    a = attn_ref[...].reshape(m, attn_ref.shape[-1]).astype(BF16)
    bc = bconv_ref[...].reshape(m, bconv_ref.shape[-1]).astype(BF16)
    mix = (jnp.dot(a, wot_ref[...], preferred_element_type=F32)
           + jnp.dot(bc, wob_ref[...], preferred_element_type=F32))
    x1 = x + gt1 * mix.reshape(nseq, rows, d)
    ms = jnp.mean(x1 * x1, axis=-1, keepdims=True)
    h2 = (x1 * lax.rsqrt(ms + EPS)) * g2_ref[...][None]
    h2 = h2 * (1.0 + sc2) + sh2
    hb = h2.reshape(m, d).astype(BF16)

    if carry:
        i = pl.program_id(1)

        @pl.when(i == 0)
        def _():
            gbuf_ref[:, 6:8, :] = jnp.zeros((nseq, 2, f), F32)

        @pl.when(i > 0)
        def _():
            gbuf_ref[:, 6:8, :] = gbuf_ref[:, rows + 6:rows + 8, :]
    else:
        gbuf_ref[:, 6:8, :] = prev_ref[...]

    acc = jnp.zeros((m, d), F32)
    for fc in range(f // fchunk):
        sl = slice(fc * fchunk, (fc + 1) * fchunk)
        g = jnp.dot(hb, wg_ref[:, sl], preferred_element_type=F32).reshape(nseq, rows, fchunk)
        u = jnp.dot(hb, wu_ref[:, sl], preferred_element_type=F32).reshape(nseq, rows, fchunk)
        gbuf_ref[:, 8:, sl] = g
        wfc = wfc_ref[:, sl]
        gc = wfc[0:1, :][None] * gbuf_ref[:, 6:6 + rows, sl]
        gc = gc + wfc[1:2, :][None] * gbuf_ref[:, 7:7 + rows, sl]
        gc = gc + wfc[2:3, :][None] * g
        z = gc + bfc_ref[:, sl][None]
        act = (z * jax.nn.sigmoid(z)) * u
        acc = acc + jnp.dot(act.reshape(m, fchunk).astype(BF16), wd_ref[sl, :], preferred_element_type=F32)
        fst_ref[:, :, sl] = g[:, rows - 2:rows, :]
    y_ref[...] = x1 + gt2 * acc.reshape(nseq, rows, d)


def _ffn(x, mod, attn, bconv, prev, wts, *, carry, rows, nseq, fchunk):
    nb, t, d = x.shape
    f = wts["wg"].shape[1]
    if carry:
        grid = (nb, t // rows)
        tile = lambda c: pl.BlockSpec((1, rows, c), lambda b, i: (b, i, 0))
        per_seq = lambda r, c: pl.BlockSpec((1, r, c), lambda b, i: (b, 0, 0))
    else:
        grid = (1, nb // nseq)
        tile = lambda c: pl.BlockSpec((nseq, rows, c), lambda b, i: (i, 0, 0))
        per_seq = lambda r, c: pl.BlockSpec((nseq, r, c), lambda b, i: (i, 0, 0))
    in_specs = [tile(d), per_seq(1, mod.shape[-1]), tile(attn.shape[-1]), tile(bconv.shape[-1]),
                _const_spec(wts["wot"].shape), _const_spec(wts["wob"].shape), _const_spec((1, d)),
                _const_spec(wts["wg"].shape), _const_spec(wts["wu"].shape), _const_spec((CONV_K, f)),
                _const_spec((1, f)), _const_spec(wts["wd"].shape), per_seq(CONV_K - 1, f)]
    return pl.pallas_call(
        functools.partial(_ffn_kernel, carry=carry, fchunk=fchunk),
        grid=grid, in_specs=in_specs,
        out_specs=[tile(d), per_seq(CONV_K - 1, f)],
        out_shape=[jax.ShapeDtypeStruct((nb, t, d), F32), jax.ShapeDtypeStruct((nb, CONV_K - 1, f), F32)],
        scratch_shapes=[pltpu.VMEM((nseq, rows + 8, f), F32)],
        compiler_params=pltpu.CompilerParams(dimension_semantics=("arbitrary", "arbitrary"),
                                             vmem_limit_bytes=VMEM_LIMIT),
        name="ffn_prompt" if carry else "ffn_sample",
    )(x, mod, attn, bconv, wts["wot"], wts["wob"], wts["g2"], wts["wg"], wts["wu"], wts["wfc"], wts["bfc"],
      wts["wd"], prev)


def _rope_tables(pos):
    half = ROPE_DIM // 2
    inv = ROPE_THETA ** (-jnp.arange(0, ROPE_DIM, 2, dtype=F32) / ROPE_DIM)
    ang = pos.astype(F32)[:, None] * inv[None, :]
    cos = jnp.cos(ang)
    sin = jnp.sin(ang)
    t = pos.shape[0]
    pad1 = jnp.ones((t, HEAD_DIM - ROPE_DIM), F32)
    pad0 = jnp.zeros((t, HEAD_DIM - ROPE_DIM), F32)
    zh = jnp.zeros((t, half), F32)
    c64 = jnp.concatenate([cos, cos, pad1], axis=1)
    a64 = jnp.concatenate([-sin, zh, pad0], axis=1)
    b64 = jnp.concatenate([zh, sin, pad0], axis=1)
    rep = LANES // HEAD_DIM
    return tuple(jnp.tile(x, (1, rep)) for x in (c64, a64, b64))


def _layer_weights(l, d, g_mix_norm, w_in, g_q, g_k, w_short_conv, w_o, g_ffn_norm, w_gate, w_up, w_ffn_conv,
                   b_ffn_conv, w_down):
    aw = d // 2
    cw = d - aw
    nh = aw // HEAD_DIM
    w = w_in[l]
    nidx = N_IDX_HEADS * IDX_DIM + IDX_DIM + N_IDX_HEADS
    o_idx = 3 * aw
    o_cv = o_idx + nidx
    widx = jnp.pad(w[:, o_idx:o_cv], ((0, 0), (0, IDX_PAD - nidx)))
    wih, wil = _split_bf16(widx)
    head = jnp.arange(aw) // HEAD_DIM
    bd = jnp.where(head[:, None] == head[None, :], 1.0 / HEAD_DIM, 0.0).astype(BF16)
    return dict(
        gmix=g_mix_norm[l].reshape(1, d),
        wqkv=w[:, 0:3 * aw].astype(BF16), wih=wih, wil=wil,
        wcv=w[:, o_cv:o_cv + 3 * cw].astype(BF16), bd=bd,
        gq=jnp.tile(g_q[l], nh).reshape(1, aw), gk=jnp.tile(g_k[l], nh).reshape(1, aw),
        wsc=w_short_conv[l],
        wot=w_o[l][0:aw].astype(BF16), wob=w_o[l][aw:].astype(BF16),
        g2=g_ffn_norm[l].reshape(1, d),
        wg=w_gate[l].astype(BF16), wu=w_up[l].astype(BF16), wfc=w_ffn_conv[l],
        bfc=b_ffn_conv[l].reshape(1, -1), wd=w_down[l].astype(BF16))


def _pick(n, candidates):
    for c in candidates:
        if n % c == 0:
            return c
    raise ValueError(f"no tile size in {candidates} divides {n}")


def kernel(x_prompt, x_sample, cache_k, cache_v, cache_kidx, state_conv, state_ffn_conv, page_table, c_prompt,
           c_sample, w_ada, b_ada, g_mix_norm, w_in, g_q, g_k, w_short_conv, w_o, g_ffn_norm, w_gate, w_up,
           w_ffn_conv, b_ffn_conv, w_down):
    nbp, seq, d = x_prompt.shape
    nbs, t_new, _ = x_sample.shape
    depth = w_in.shape[0]
    aw = d // 2
    cw = d - aw
    nh = aw // HEAD_DIM
    f = w_gate.shape[-1]
    npages = page_table.shape[1]
    past = npages * PAGE_SIZE
    n_pool = cache_k.shape[1]
    assert t_new == 8 and cache_k.shape[2] == PAGE_SIZE and aw % LANES == 0 and nh % 2 == 0

    rows_in = _pick(seq, (512, 256, 128))
    rows_ffn = _pick(seq, (256, 128))
    blk = _pick(seq, (256, 128))
    nseq = _pick(nbs, (32, 16, 8, 4, 2, 1))
    ppc_idx = _pick(npages, (16, 8, 4, 2, 1))
    ppc_att = _pick(npages, (8, 4, 2, 1))
    fchunk = f // 2 if (f // 2) % LANES == 0 else f
    top_p = min(TOPK_MAX, seq // TOPK_FRAC)
    top_s = min(TOPK_MAX, (past + t_new) // TOPK_FRAC)

    tabs_p = _rope_tables(jnp.arange(seq))
    tabs_s = _rope_tables(past + jnp.arange(t_new))
    zeros_conv = jnp.zeros((nbp, CONV_K - 1, cw), F32)
    zeros_ffn = jnp.zeros((nbp, CONV_K - 1, f), F32)

    xp, xs = x_prompt, x_sample
    outs = {k: [] for k in ("kp", "vp", "kip", "cp", "fp", "ks", "vs", "kis", "cs", "fs")}
    for l in range(depth):
        wts = _layer_weights(l, d, g_mix_norm, w_in, g_q, g_k, w_short_conv, w_o, g_ffn_norm, w_gate, w_up,
                             w_ffn_conv, b_ffn_conv, w_down)
        mod = _ada(jnp.concatenate([c_prompt, c_sample], axis=0), w_ada[l], b_ada[l])
        mod_p = mod[:nbp].reshape(nbp, 1, 6 * d)
        mod_s = mod[nbp:].reshape(nbs, 1, 6 * d)

        p = _inproj(xp, mod_p, zeros_conv, wts, tabs_p, carry=True, rows=rows_in, nseq=1, act_dtype=BF16)
        attn_p = _pattn(p, top=top_p, blk=blk)
        xp, fst_p = _ffn(xp, mod_p, attn_p, p["bconv"], zeros_ffn, wts, carry=True, rows=rows_ffn, nseq=1,
                         fchunk=fchunk)

        s = _inproj(xs, mod_s, state_conv[l], wts, tabs_s, carry=False, rows=t_new, nseq=nseq, act_dtype=F32)
        bias = _sidx(s, cache_kidx[l].reshape(n_pool, PAGE_SIZE, IDX_DIM), page_table, top=top_s, ppc=ppc_idx)
        attn_s = _sattn(s, bias, cache_k[l].reshape(n_pool, PAGE_SIZE, aw), cache_v[l].reshape(n_pool, PAGE_SIZE, aw),
                        page_table, ppc=ppc_att)
        xs, fst_s = _ffn(xs, mod_s, attn_s, s["bconv"], state_ffn_conv[l], wts, carry=False, rows=t_new, nseq=nseq,
                         fchunk=fchunk)

        outs["kp"].append(p["k"].reshape(nbp, seq, nh, HEAD_DIM))
        outs["vp"].append(p["v"].reshape(nbp, seq, nh, HEAD_DIM))
        outs["kip"].append(p["kidx"])
        outs["cp"].append(p["cstate"])
        outs["fp"].append(fst_p)
        outs["ks"].append(s["k"].reshape(nbs, t_new, nh, HEAD_DIM))
        outs["vs"].append(s["v"].reshape(nbs, t_new, nh, HEAD_DIM))
        outs["kis"].append(s["kidx"])
        outs["cs"].append(s["cstate"])
        outs["fs"].append(fst_s)

    st = lambda k: jnp.stack(outs[k])
    return (xp, xs, st("kp"), st("vp"), st("kip"), st("cp"), st("fp"),
            st("ks"), st("vs"), st("kis"), st("cs"), st("fs"))
```

```python
import functools
import math

import jax
import jax.numpy as jnp
from jax import lax
from jax.experimental import pallas as pl
from jax.experimental.pallas import tpu as pltpu

F32 = jnp.float32
BF16 = jnp.bfloat16
I32 = jnp.int32

HEAD_DIM = 64
IDX_DIM = 64
N_IDX_HEADS = 4
ROPE_DIM = 16
ROPE_HALF = ROPE_DIM // 2
ROPE_THETA = 500000.0
CONV_K = 3
TOPK_MAX = 256
TOPK_FRAC = 4
PAGE_SIZE = 128
EPS = 1e-6

LANES = 128
SUBLANES = 8
IDX_PAD = 384
QI3 = 256
NEG_BIAS = -1e30
INT_MIN = -(2 ** 31)
VMEM_LIMIT = 56 * 1024 * 1024
LOG2E = math.log2(math.e)

_NT = (((1,), (1,)), ((), ()))


def _sortable(x):
    b = lax.bitcast_convert_type(x, I32)
    return b ^ (lax.shift_right_arithmetic(b, 31) & 0x7FFFFFFF)


_NEG_INF_KEY = (0xFF800000 ^ 0x7FFFFFFF) - 2 ** 32


def _lane_low(shape):
    return (lax.broadcasted_iota(I32, shape, len(shape) - 1) % LANES) < (LANES // 2)


def _split_bf16(x):
    hi = x.astype(BF16)
    lo = (x - hi.astype(F32)).astype(BF16)
    return hi, lo


MAX_PREDS = 3


def _one_if(cond):
    return jnp.where(cond, 1.0, 0.0)


def _bit_search(count, nbits, start, ones, pred_of, accept):
    t = start
    if nbits % 2:
        cand = t | jnp.left_shift(ones, nbits - 1)
        (n,) = count([pred_of(cand)])
        t = jnp.where(accept(n), cand, t)
    npair = nbits // 2

    def step(i, t):
        sh = 2 * (npair - 1 - i)
        cands = [t | jnp.left_shift(ones * a, sh) for a in (1, 2, 3)]
        ns = count([pred_of(c) for c in cands])
        for c, n in zip(cands, ns):
            t = jnp.where(accept(n), c, t)
        return t

    return lax.fori_loop(0, npair, step, t)


def _select_params(count, top, nbits_idx, ones):
    kf = float(top)
    zero = ones * 0
    (c0,) = count([lambda kk, idx, rs: _one_if(kk >= 0)])
    t0 = jnp.where(c0 >= kf, zero, zero + INT_MIN)
    t = _bit_search(count, 31, t0, ones, lambda cand: (lambda kk, idx, rs: _one_if(kk >= cand[rs])),
                    lambda n: n >= kf)
    cgt, cge = count([lambda kk, idx, rs: _one_if(kk > t[rs]), lambda kk, idx, rs: _one_if(kk >= t[rs])])
    r = kf - cgt

    def jsearch():
        return _bit_search(
            count, nbits_idx, zero, ones,
            lambda cand: (lambda kk, idx, rs: jnp.where(kk == t[rs], _one_if(idx < cand[rs]), 0.0)),
            lambda n: n < r)

    need = jnp.max(cge) > kf
    j = lax.cond(need, jsearch, lambda: zero + (2 ** nbits_idx))
    return t, j


def _bias_tile(kk, idx, t, j):
    tie = jnp.where(kk == t, jnp.where(idx <= j, 0.0, NEG_BIAS), NEG_BIAS)
    bias = jnp.where(kk > t, 0.0, tie)
    return jnp.where(kk == _NEG_INF_KEY, NEG_BIAS, bias)


def _const_spec(shape):
    nd = len(shape)
    return pl.BlockSpec(shape, lambda *_: (0,) * nd, pipeline_mode=pl.Buffered(1))


def _params(n_grid):
    return pltpu.CompilerParams(dimension_semantics=("arbitrary",) * n_grid, vmem_limit_bytes=VMEM_LIMIT)


def _ada_kernel(c_ref, w_ref, b_ref, o_ref):
    c = c_ref[...]
    a = c * jax.nn.sigmoid(c)
    o_ref[...] = jnp.dot(a, w_ref[...], preferred_element_type=F32,
                         precision=lax.Precision.HIGHEST) + b_ref[...]


def _ada(c_all, w_ada, b_ada):
    n, d = c_all.shape
    n6 = w_ada.shape[1]
    return pl.pallas_call(
        _ada_kernel,
        grid=(n6 // d,),
        in_specs=[pl.BlockSpec((n, d), lambda j: (0, 0)),
                  pl.BlockSpec((d, d), lambda j: (0, j)),
                  pl.BlockSpec((1, d), lambda j: (0, j))],
        out_specs=pl.BlockSpec((n, d), lambda j: (0, j)),
        out_shape=jax.ShapeDtypeStruct((n, n6), F32),
        compiler_params=_params(1),
        name="ada",
    )(c_all, w_ada, b_ada.reshape(1, n6))


def _modulated_norm(x, mod, g, d):
    ms = jnp.mean(x * x, axis=-1, keepdims=True)
    h = (x * lax.rsqrt(ms + EPS)) * g[None]
    return h * (1.0 + mod[:, :, d:2 * d]) + mod[:, :, 0:d]


def _rope_lanes(x2, nseq, rows, cos, sa, sb):
    m, w = x2.shape
    xm = pltpu.roll(x2, w - ROPE_HALF, 1)
    xp = pltpu.roll(x2, ROPE_HALF, 1)
    y = (x2.reshape(nseq, rows, w) * cos[None] + xm.reshape(nseq, rows, w) * sa[None]
         + xp.reshape(nseq, rows, w) * sb[None])
    return y.reshape(m, w)


def _rope_rows(x3, cos, sin):
    x1 = x3[:, 0:ROPE_HALF, :]
    x2 = x3[:, ROPE_HALF:ROPE_DIM, :]
    return jnp.concatenate([x1 * cos[None] - x2 * sin[None], x2 * cos[None] + x1 * sin[None],
                            x3[:, ROPE_DIM:, :]], axis=1)


def _q_outputs(q, pidx, nseq, rows, tabs, bd_ref, gq, qexp_ref, qi3_ref, wi_ref):
    m, aw = q.shape
    nh = aw // HEAD_DIM
    cos, sa, sb = tabs
    reps = aw // LANES
    ss = jnp.dot((q * q).astype(BF16), bd_ref[...], preferred_element_type=F32)
    q = (q * lax.rsqrt(ss + EPS)) * gq
    q = _rope_lanes(q, nseq, rows, jnp.concatenate([cos] * reps, axis=1), jnp.concatenate([sa] * reps, axis=1),
                    jnp.concatenate([sb] * reps, axis=1))
    low = _lane_low((1, LANES))
    qs = q * (HEAD_DIM ** -0.5 * LOG2E)
    pieces = []
    for hh in range(nh):
        chunk = qs[:, (hh // 2) * LANES:(hh // 2 + 1) * LANES]
        keep = low if hh % 2 == 0 else jnp.logical_not(low)
        pieces.append(jnp.where(keep, chunk, 0.0))
    qexp_ref[...] = jnp.concatenate(pieces, axis=1).reshape(nseq, rows, nh * LANES).astype(qexp_ref.dtype)

    nqi = N_IDX_HEADS * IDX_DIM
    qi = _rope_lanes(pidx[:, 0:nqi], nseq, rows, jnp.concatenate([cos] * (nqi // LANES), axis=1),
                     jnp.concatenate([sa] * (nqi // LANES), axis=1), jnp.concatenate([sb] * (nqi // LANES), axis=1))
    wi_ref[...] = (pidx[:, nqi:nqi + LANES] * ((IDX_DIM ** -0.5) * (N_IDX_HEADS ** -0.5))).reshape(nseq, rows, LANES)
    zero = jnp.zeros((m, LANES), F32)
    q3 = []
    for t in range(nqi // LANES):
        xt = qi[:, t * LANES:(t + 1) * LANES]
        hi = xt.astype(BF16).astype(F32)
        lo = xt - hi
        hi_r = pltpu.roll(hi, LANES // 2, 1)
        lo_r = pltpu.roll(lo, LANES // 2, 1)
        q3 += [jnp.where(low, hi, hi_r), jnp.where(low, lo, zero),
               jnp.where(low, hi_r, hi), jnp.where(low, lo_r, zero)]
    qi3_ref[...] = jnp.concatenate(q3, axis=1).reshape(nseq, rows, N_IDX_HEADS * QI3).astype(qi3_ref.dtype)


def _conv_branch(pcv, nseq, rows, cw, wsc_ref, cbuf_ref, bconv_ref, cst_ref):
    bg = pcv[:, 0:cw].reshape(nseq, rows, cw)
    cu = (pcv[:, cw:2 * cw] * pcv[:, 2 * cw:3 * cw]).reshape(nseq, rows, cw)
    cbuf_ref[:, 8:, :] = cu
    wsc = wsc_ref[...]
    conv = wsc[0:1, :][None] * cbuf_ref[:, 6:6 + rows, :]
    conv = conv + wsc[1:2, :][None] * cbuf_ref[:, 7:7 + rows, :]
    conv = conv + wsc[2:3, :][None] * cu
    bconv_ref[...] = (bg * conv).astype(bconv_ref.dtype)
    cst_ref[...] = cu[:, rows - 2:rows, :]


def _inproj_p_kernel(x_ref, mod_ref, gmix_ref, wq_ref, wih_ref, wil_ref, wcv_ref, wkv_ref, wkil_ref, bd_ref,
                     gq_ref, gkc_ref, cos_ref, sa_ref, sb_ref, cost_ref, sint_ref, wsc_ref,
                     qexp_ref, qi3_ref, wi_ref, bconv_ref, cst_ref, kt_ref, vt_ref, kit_ref, ktb_ref, vta_ref,
                     ki3t_ref, cbuf_ref):
    _, rows, d = x_ref.shape
    aw = kt_ref.shape[1]
    cw = bconv_ref.shape[-1]
    nh = aw // HEAD_DIM
    nblk, _, blk = ktb_ref.shape[1:]

    h2 = _modulated_norm(x_ref[...], mod_ref[...], gmix_ref[...], d).reshape(rows, d)
    hb, hl = _split_bf16(h2)
    tabs = (cos_ref[...], sa_ref[...], sb_ref[...])

    q = jnp.dot(hb, wq_ref[...], preferred_element_type=F32)
    pidx = (jnp.dot(hb, wih_ref[...], preferred_element_type=F32)
            + jnp.dot(hl, wih_ref[...], preferred_element_type=F32)
            + jnp.dot(hb, wil_ref[...], preferred_element_type=F32))
    _q_outputs(q, pidx, 1, rows, tabs, bd_ref, gq_ref[...], qexp_ref, qi3_ref, wi_ref)

    kvt = lax.dot_general(wkv_ref[...], hb, _NT, preferred_element_type=F32)
    cost = cost_ref[...]
    sint = sint_ref[...]
    k3 = kvt[0:aw].reshape(nh, HEAD_DIM, rows)
    ss = jnp.mean(k3 * k3, axis=1, keepdims=True)
    k3 = (k3 * lax.rsqrt(ss + EPS)) * gkc_ref[...][None]
    kt = _rope_rows(k3, cost, sint).reshape(aw, rows)
    vt = kvt[aw:2 * aw]
    kit = (kvt[2 * aw:2 * aw + IDX_DIM]
           + lax.dot_general(wkv_ref[2 * aw:2 * aw + IDX_DIM, :], hl, _NT, preferred_element_type=F32)
           + lax.dot_general(wkil_ref[...], hb, _NT, preferred_element_type=F32))
    kit = _rope_rows(kit.reshape(1, IDX_DIM, rows), cost, sint).reshape(IDX_DIM, rows)
    kt_ref[0] = kt
    vt_ref[0] = vt
    kit_ref[0] = kit

    kih = kit.astype(BF16)
    kil = (kit - kih.astype(F32)).astype(BF16)
    ki3t = jnp.concatenate([kih, kil, kih, jnp.zeros((QI3 - 3 * IDX_DIM, rows), BF16)], axis=0)
    ktb = kt.astype(BF16)
    vtb = vt.astype(BF16)
    ones = jnp.ones((HEAD_DIM, rows), BF16)
    vta = jnp.concatenate(
        [blk_ for hh in range(nh) for blk_ in
         ((vtb[hh * HEAD_DIM:(hh + 1) * HEAD_DIM], ones) if hh % 2 == 0 else
          (ones, vtb[hh * HEAD_DIM:(hh + 1) * HEAD_DIM]))], axis=0)
    for bb in range(nblk):
        cs = slice(bb * blk, (bb + 1) * blk)
        ktb_ref[0, bb] = ktb[:, cs]
        vta_ref[0, bb] = vta[:, cs]
        ki3t_ref[0, bb] = ki3t[:, cs]

    i = pl.program_id(1)

    @pl.when(i == 0)
    def _():
        cbuf_ref[:, 6:8, :] = jnp.zeros((1, 2, cw), F32)

    @pl.when(i > 0)
    def _():
        cbuf_ref[:, 6:8, :] = cbuf_ref[:, rows + 6:rows + 8, :]

    _conv_branch(jnp.dot(hb, wcv_ref[...], preferred_element_type=F32), 1, rows, cw, wsc_ref, cbuf_ref,
                 bconv_ref, cst_ref)


def _inproj_prompt(x, mod, wts, tabs, tabs_t, *, rows, blk):
    nb, t, d = x.shape
    aw = d // 2
    cw = d - aw
    nh = aw // HEAD_DIM
    nblk = rows // blk
    tile = lambda c: pl.BlockSpec((1, rows, c), lambda b, i: (b, i, 0))
    per_seq = lambda r, c: pl.BlockSpec((1, r, c), lambda b, i: (b, 0, 0))
    ftile = lambda r: pl.BlockSpec((1, r, rows), lambda b, i: (b, 0, i))
    btile = lambda r: pl.BlockSpec((1, nblk, r, blk), lambda b, i: (b, i, 0, 0))
    tab = pl.BlockSpec((rows, LANES), lambda b, i: (i, 0))
    tab_t = pl.BlockSpec((ROPE_HALF, rows), lambda b, i: (0, i))
    cos, sa, sb = tabs
    cost, sint = tabs_t
    in_specs = [tile(d), per_seq(1, mod.shape[-1]), _const_spec((1, d)),
                _const_spec(wts["wq"].shape), _const_spec(wts["wih"].shape), _const_spec(wts["wil"].shape),
                _const_spec(wts["wcv"].shape), _const_spec(wts["wkv"].shape), _const_spec(wts["wkil"].shape),
                _const_spec(wts["bd"].shape), _const_spec((1, aw)), _const_spec((HEAD_DIM, 1)),
                tab, tab, tab, tab_t, tab_t, _const_spec((CONV_K, cw))]
    out_specs = [tile(nh * LANES), tile(N_IDX_HEADS * QI3), tile(LANES), tile(cw), per_seq(CONV_K - 1, cw),
                 ftile(aw), ftile(aw), ftile(IDX_DIM), btile(aw), btile(2 * aw), btile(QI3)]
    sds = jax.ShapeDtypeStruct
    out_shape = [sds((nb, t, nh * LANES), BF16), sds((nb, t, N_IDX_HEADS * QI3), BF16), sds((nb, t, LANES), F32),
                 sds((nb, t, cw), BF16), sds((nb, CONV_K - 1, cw), F32),
                 sds((nb, aw, t), F32), sds((nb, aw, t), F32), sds((nb, IDX_DIM, t), F32),
                 sds((nb, t // blk, aw, blk), BF16), sds((nb, t // blk, 2 * aw, blk), BF16),
                 sds((nb, t // blk, QI3, blk), BF16)]
    names = ["qexp", "qi3", "wi", "bconv", "cstate", "kt", "vt", "kit", "ktb", "vta", "ki3t"]
    res = pl.pallas_call(
        _inproj_p_kernel,
        grid=(nb, t // rows), in_specs=in_specs, out_specs=out_specs, out_shape=out_shape,
        scratch_shapes=[pltpu.VMEM((1, rows + 8, cw), F32)],
        compiler_params=_params(2),
        name="inproj_prompt",
    )(x, mod, wts["gmix"], wts["wq"], wts["wih"], wts["wil"], wts["wcv"], wts["wkv"], wts["wkil"], wts["bd"],
      wts["gq"], wts["gkc"], cos, sa, sb, cost, sint, wts["wsc"])
    return dict(zip(names, res))


def _inproj_s_kernel(x_ref, mod_ref, gmix_ref, wq_ref, wih_ref, wil_ref, wcv_ref, wkvs_ref, wkih_ref, wkil_ref,
                     bd_ref, gq_ref, gk_ref, cos_ref, sa_ref, sb_ref, wsc_ref, prev_ref,
                     qexp_ref, qi3_ref, wi_ref, bconv_ref, cst_ref, k_ref, v_ref, kidx_ref, ki3_ref, cbuf_ref):
    nseq, rows, d = x_ref.shape
    m = nseq * rows
    aw = k_ref.shape[-1]
    cw = bconv_ref.shape[-1]
    h2 = _modulated_norm(x_ref[...], mod_ref[...], gmix_ref[...], d).reshape(m, d)
    hb, hl = _split_bf16(h2)
    cos, sa, sb = cos_ref[...], sa_ref[...], sb_ref[...]

    q = jnp.dot(hb, wq_ref[...], preferred_element_type=F32)
    pidx = (jnp.dot(hb, wih_ref[...], preferred_element_type=F32)
            + jnp.dot(hl, wih_ref[...], preferred_element_type=F32)
            + jnp.dot(hb, wil_ref[...], preferred_element_type=F32))
    _q_outputs(q, pidx, nseq, rows, (cos, sa, sb), bd_ref, gq_ref[...], qexp_ref, qi3_ref, wi_ref)

    kv = jnp.dot(hb, wkvs_ref[...], preferred_element_type=F32)
    k = kv[:, 0:aw]
    ss = jnp.dot((k * k).astype(BF16), bd_ref[...], preferred_element_type=F32)
    k = (k * lax.rsqrt(ss + EPS)) * gk_ref[...]
    reps = aw // LANES
    k = _rope_lanes(k, nseq, rows, jnp.concatenate([cos] * reps, axis=1), jnp.concatenate([sa] * reps, axis=1),
                    jnp.concatenate([sb] * reps, axis=1))
    k_ref[...] = k.reshape(nseq, rows, aw)
    v_ref[...] = kv[:, aw:2 * aw].reshape(nseq, rows, aw)

    ki = (jnp.dot(hb, wkih_ref[...], preferred_element_type=F32)
          + jnp.dot(hl, wkih_ref[...], preferred_element_type=F32)
          + jnp.dot(hb, wkil_ref[...], preferred_element_type=F32))
    low = _lane_low((1, LANES))
    ki = _rope_lanes(ki, nseq, rows, jnp.where(low, cos, 1.0), jnp.where(low, sa, 0.0), jnp.where(low, sb, 0.0))
    kidx_ref[...] = ki[:, 0:IDX_DIM].reshape(nseq, rows, IDX_DIM)
    khi = ki.astype(BF16).astype(F32)
    klo = ki - khi
    zero = jnp.zeros((m, LANES), F32)
    k3 = jnp.concatenate([jnp.where(low, khi, pltpu.roll(klo, LANES // 2, 1)), jnp.where(low, khi, zero)], axis=1)
    ki3_ref[...] = k3.reshape(nseq, rows, QI3)

    cbuf_ref[:, 6:8, :] = prev_ref[...]
    _conv_branch(jnp.dot(hb, wcv_ref[...], preferred_element_type=F32), nseq, rows, cw, wsc_ref, cbuf_ref,
                 bconv_ref, cst_ref)


def _inproj_sample(x, mod, prev, wts, tabs, *, nseq):
    nb, rows, d = x.shape
    aw = d // 2
    cw = d - aw
    nh = aw // HEAD_DIM
    tile = lambda c: pl.BlockSpec((nseq, rows, c), lambda i: (i, 0, 0))
    per_seq = lambda r, c: pl.BlockSpec((nseq, r, c), lambda i: (i, 0, 0))
    tab = pl.BlockSpec((rows, LANES), lambda i: (0, 0))
    cos, sa, sb = tabs
    in_specs = [tile(d), per_seq(1, mod.shape[-1]), _const_spec((1, d)),
                _const_spec(wts["wq"].shape), _const_spec(wts["wih"].shape), _const_spec(wts["wil"].shape),
                _const_spec(wts["wcv"].shape), _const_spec(wts["wkvs"].shape), _const_spec(wts["wkih"].shape),
                _const_spec(wts["wkils"].shape), _const_spec(wts["bd"].shape), _const_spec((1, aw)),
                _const_spec((1, aw)), tab, tab, tab, _const_spec((CONV_K, cw)), per_seq(CONV_K - 1, cw)]
    cols = [("qexp", nh * LANES), ("qi3", N_IDX_HEADS * QI3), ("wi", LANES), ("bconv", cw)]
    cols2 = [("k", aw), ("v", aw), ("kidx", IDX_DIM), ("ki3", QI3)]
    out_specs = ([tile(c) for _, c in cols] + [per_seq(CONV_K - 1, cw)] + [tile(c) for _, c in cols2])
    sds = jax.ShapeDtypeStruct
    out_shape = ([sds((nb, rows, c), F32) for _, c in cols] + [sds((nb, CONV_K - 1, cw), F32)]
                 + [sds((nb, rows, c), F32) for _, c in cols2])
    names = [n for n, _ in cols] + ["cstate"] + [n for n, _ in cols2]
    res = pl.pallas_call(
        _inproj_s_kernel,
        grid=(nb // nseq,), in_specs=in_specs, out_specs=out_specs, out_shape=out_shape,
        scratch_shapes=[pltpu.VMEM((nseq, rows + 8, cw), F32)],
        compiler_params=_params(1),
        name="inproj_sample",
    )(x, mod, wts["gmix"], wts["wq"], wts["wih"], wts["wil"], wts["wcv"], wts["wkvs"], wts["wkih"], wts["wkils"],
      wts["bd"], wts["gq"], wts["gk"], cos, sa, sb, wts["wsc"], prev)
    return dict(zip(names, res))


def _pattn_kernel(qexp_ref, qi3_ref, wi_ref, ki3t_ref, ktb_ref, vta_ref, o_ref, keys_ref, cnt_ref, m_ref, acc_ref,
                  *, top, blk, seq):
    nh = acc_ref.shape[0]
    nlt = blk // LANES
    nrg = blk // SUBLANES
    qb = pl.program_id(1)
    nkb = qb + 1
    q0 = qb * blk
    wi = wi_ref[0]
    row = lax.broadcasted_iota(I32, (blk, blk), 0)
    col = lax.broadcasted_iota(I32, (blk, blk), 1)
    lane = lax.broadcasted_iota(I32, (SUBLANES, LANES), 1)
    ones = jnp.ones((blk, LANES), I32)
    nidx = N_IDX_HEADS * IDX_DIM

    def score_body(kb, carry_):
        kblk = ki3t_ref[0, kb]
        acc = jnp.zeros((blk, blk), F32)
        for hh in range(N_IDX_HEADS):
            dd = jnp.dot(qi3_ref[0, :, hh * QI3:(hh + 1) * QI3], kblk, preferred_element_type=F32)
            acc = acc + jnp.maximum(dd, 0.0) * wi[:, hh:hh + 1]
        acc = jnp.where(acc == 0.0, 0.0, acc)
        acc = jnp.where(kb * blk + col <= q0 + row, acc, -jnp.inf)
        keys_ref[kb] = _sortable(acc)
        return carry_

    lax.fori_loop(0, nkb, score_body, 0)

    def count(preds):
        na = len(preds)
        cnt_ref[0:na] = jnp.zeros((na,) + cnt_ref.shape[1:], F32)

        def body(kb, carry_):
            for rg in range(nrg):
                rs = slice(rg * SUBLANES, (rg + 1) * SUBLANES)
                cs = [cnt_ref[a, rs, :] for a in range(na)]
                for jj in range(nlt):
                    kk = keys_ref[kb, rs, jj * LANES:(jj + 1) * LANES]
                    idx = kb * blk + jj * LANES + lane
                    cs = [c + pred(kk, idx, rs) for c, pred in zip(cs, preds)]
                for a in range(na):
                    cnt_ref[a, rs, :] = cs[a]
            return carry_

        lax.fori_loop(0, nkb, body, 0)
        return [jnp.sum(cnt_ref[a], axis=1, keepdims=True) for a in range(na)]

    t, j = _select_params(count, top, (seq - 1).bit_length(), ones)

    m_ref[...] = jnp.full(m_ref.shape, -jnp.inf, F32)
    acc_ref[...] = jnp.zeros(acc_ref.shape, F32)
    lane_b = lax.broadcasted_iota(I32, (blk, LANES), 1)

    def attn_body(kb, carry_):
        bias = jnp.concatenate(
            [_bias_tile(keys_ref[kb, :, jj * LANES:(jj + 1) * LANES], kb * blk + jj * LANES + lane_b, t, j)
             for jj in range(nlt)], axis=1)
        for hh in range(nh):
            pp = hh // 2
            qh = qexp_ref[0, :, hh * LANES:(hh + 1) * LANES]
            kh = ktb_ref[0, kb, pp * LANES:(pp + 1) * LANES, :]
            va = vta_ref[0, kb, hh * LANES:(hh + 1) * LANES, :]
            s = jnp.dot(qh, kh, preferred_element_type=F32) + bias
            m_prev = m_ref[hh]
            m_new = jnp.maximum(m_prev, jnp.max(s, axis=1, keepdims=True))
            alpha = jnp.exp2(m_prev - m_new)
            p = jnp.exp2(s - jnp.concatenate([m_new] * nlt, axis=1))
            acc_ref[hh] = alpha * acc_ref[hh] + lax.dot_general(p.astype(BF16), va, _NT, preferred_element_type=F32)
            m_ref[hh] = m_new
        return carry_

    lax.fori_loop(0, nkb, attn_body, 0)

    low = _lane_low((1, LANES))
    outs = []
    for pp in range(nh // 2):
        a0 = acc_ref[2 * pp]
        a1 = acc_ref[2 * pp + 1]
        outs.append(jnp.where(low, a0 / pltpu.roll(a0, LANES // 2, 1), a1 / pltpu.roll(a1, LANES // 2, 1)))
    o_ref[0] = jnp.concatenate(outs, axis=1).astype(o_ref.dtype)


def _pattn(p, *, top, blk):
    nb, nblk, aw, _ = p["ktb"].shape
    seq = nblk * blk
    nh = aw // HEAD_DIM
    qtile = lambda c: pl.BlockSpec((1, blk, c), lambda b, i: (b, i, 0))
    whole = lambda r: pl.BlockSpec((1, nblk, r, blk), lambda b, i: (b, 0, 0, 0))
    return pl.pallas_call(
        functools.partial(_pattn_kernel, top=top, blk=blk, seq=seq),
        grid=(nb, nblk),
        in_specs=[qtile(nh * LANES), qtile(N_IDX_HEADS * QI3), qtile(LANES), whole(QI3), whole(aw), whole(2 * aw)],
        out_specs=qtile(aw),
        out_shape=jax.ShapeDtypeStruct((nb, seq, aw), BF16),
        scratch_shapes=[pltpu.VMEM((nblk, blk, blk), I32),
                        pltpu.VMEM((MAX_PREDS, blk, LANES), F32),
                        pltpu.VMEM((nh, blk, LANES), F32),
                        pltpu.VMEM((nh, blk, LANES), F32)],
        compiler_params=_params(2),
        name="pattn",
    )(p["qexp"], p["qi3"], p["wi"], p["ki3t"], p["ktb"], p["vta"])


def _sidx_kernel(pt_ref, qi3_ref, wi_ref, kin_ref, kidx_hbm, keys_ref, buf_ref, sem_ref, *, npages):
    t_new = qi3_ref.shape[1]
    b = pl.program_id(0)
    nb = pl.num_programs(0)
    slot = lax.rem(b, 2)

    def page_copy(seq_i, i, sl):
        return pltpu.make_async_copy(kidx_hbm.at[pt_ref[seq_i, i]], buf_ref.at[sl, i], sem_ref.at[sl])

    @pl.when(b == 0)
    def _():
        for i in range(npages):
            page_copy(0, i, 0).start()

    @pl.when(b + 1 < nb)
    def _():
        for i in range(npages):
            page_copy(b + 1, i, 1 - slot).start()

    qi3 = qi3_ref[0]
    wi = wi_ref[0]
    heads = range(N_IDX_HEADS)
    lhs = jnp.concatenate([qi3[:, hh * QI3:hh * QI3 + LANES] for hh in heads]
                          + [qi3[:, hh * QI3 + LANES:(hh + 1) * QI3] for hh in heads], axis=0).astype(BF16)

    def head_sum(dd):
        acc = jnp.zeros((t_new, LANES), F32)
        for hh in heads:
            acc = acc + jnp.maximum(dd[hh * t_new:(hh + 1) * t_new], 0.0) * wi[:, hh:hh + 1]
        return jnp.where(acc == 0.0, 0.0, acc)

    for i in range(npages):
        page_copy(b, i, slot).wait()

    group = 4 if npages % 4 == 0 else 1

    def page_body(g, carry_):
        for u in range(group):
            i = g * group + u
            kp = buf_ref[slot, i]
            kh, kl = _split_bf16(kp)
            d2 = jnp.dot(lhs, jnp.concatenate([kh, kl], axis=0), preferred_element_type=F32)
            nq = N_IDX_HEADS * t_new
            keys_ref[0, i] = _sortable(head_sum(d2[0:nq] + d2[nq:2 * nq]))
        return carry_

    lax.fori_loop(0, npages // group, page_body, 0)

    q3 = jnp.concatenate([qi3[:, hh * QI3:(hh + 1) * QI3] for hh in heads], axis=0).astype(BF16)
    kn = jnp.concatenate([kin_ref[0], jnp.zeros((LANES - t_new, QI3), F32)], axis=0).astype(BF16)
    sc = head_sum(lax.dot_general(q3, kn, _NT, preferred_element_type=F32))
    rown = lax.broadcasted_iota(I32, (t_new, LANES), 0)
    lanen = lax.broadcasted_iota(I32, (t_new, LANES), 1)
    keys_ref[0, npages] = _sortable(jnp.where(lanen <= rown, sc, -jnp.inf))


def _sidx(s, kidx_t, page_table):
    nb, t_new, _ = s["qi3"].shape
    npages = page_table.shape[1]
    seq_blk = lambda c: pl.BlockSpec((1, t_new, c), lambda b, pt: (b, 0, 0))
    grid_spec = pltpu.PrefetchScalarGridSpec(
        num_scalar_prefetch=1, grid=(nb,),
        in_specs=[seq_blk(N_IDX_HEADS * QI3), seq_blk(LANES), seq_blk(QI3), pl.BlockSpec(memory_space=pl.ANY)],
        out_specs=pl.BlockSpec((1, npages + 1, t_new, LANES), lambda b, pt: (b, 0, 0, 0)),
        scratch_shapes=[pltpu.VMEM((2, npages, IDX_DIM, PAGE_SIZE), F32), pltpu.SemaphoreType.DMA((2,))])
    return pl.pallas_call(
        functools.partial(_sidx_kernel, npages=npages),
        grid_spec=grid_spec,
        out_shape=jax.ShapeDtypeStruct((nb, npages + 1, t_new, LANES), I32),
        compiler_params=_params(1),
        name="sidx",
    )(page_table, s["qi3"], s["wi"], s["ki3"], kidx_t)


def _ssel_kernel(keys_ref, bias_ref, cnt_ref, *, top):
    nseq, np1, t_new, _ = keys_ref.shape
    rows = nseq * t_new
    lane = lax.broadcasted_iota(I32, (rows, LANES), 1)
    ones = jnp.ones((rows, LANES), I32)
    cseq = math.gcd(nseq, 16)
    crow = cseq * t_new
    lane_c = lax.broadcasted_iota(I32, (crow, LANES), 1)

    def count(preds):
        na = len(preds)
        cnt_ref[0:na] = jnp.zeros((na,) + cnt_ref.shape[1:], F32)

        def body(pg, carry_):
            idx = pg * LANES + lane_c
            for ch in range(nseq // cseq):
                rs = slice(ch * crow, (ch + 1) * crow)
                kk = keys_ref[ch * cseq:(ch + 1) * cseq, pg].reshape(crow, LANES)
                for a in range(na):
                    cnt_ref[a, rs, :] = cnt_ref[a, rs, :] + preds[a](kk, idx, rs)
            return carry_

        lax.fori_loop(0, np1, body, 0)
        return [jnp.sum(cnt_ref[a], axis=1, keepdims=True) for a in range(na)]

    t, j = _select_params(count, top, (np1 * LANES - 1).bit_length(), ones)

    def bias_body(pg, carry_):
        bias_ref[:, pg] = _bias_tile(keys_ref[:, pg].reshape(rows, LANES), pg * LANES + lane, t, j).reshape(
            nseq, t_new, LANES)
        return carry_

    lax.fori_loop(0, np1, bias_body, 0)


def _ssel(keys, *, top, nseq):
    nb, np1, t_new, _ = keys.shape
    blk_spec = pl.BlockSpec((nseq, np1, t_new, LANES), lambda i: (i, 0, 0, 0))
    return pl.pallas_call(
        functools.partial(_ssel_kernel, top=top),
        grid=(nb // nseq,), in_specs=[blk_spec], out_specs=blk_spec,
        out_shape=jax.ShapeDtypeStruct(keys.shape, F32),
        scratch_shapes=[pltpu.VMEM((MAX_PREDS, nseq * t_new, LANES), F32)],
        compiler_params=_params(1),
        name="ssel",
    )(keys)


def _sattn_kernel(pt_ref, qexp_ref, bias_ref, kn_ref, vn_ref, k_hbm, v_hbm, o_ref,
                  kbuf_ref, vbuf_ref, sem_ref, qbd_ref, m_ref, l_ref, acc_ref, *, cpp, nchunks):
    t_new = kn_ref.shape[1]
    aw = kn_ref.shape[2]
    nh = aw // HEAD_DIM
    npair = nh // 2
    b = pl.program_id(0)
    c = pl.program_id(1)
    step = b * nchunks + c
    total = pl.num_programs(0) * nchunks
    slot = lax.rem(step, 2)

    def copies(seq_i, ch, sl):
        out = []
        for i in range(cpp):
            pg = pt_ref[seq_i, ch * cpp + i]
            out.append(pltpu.make_async_copy(k_hbm.at[pg], kbuf_ref.at[sl, i], sem_ref.at[0, sl]))
            out.append(pltpu.make_async_copy(v_hbm.at[pg], vbuf_ref.at[sl, i], sem_ref.at[1, sl]))
        return out

    @pl.when(step == 0)
    def _():
        for cp in copies(0, 0, 0):
            cp.start()

    @pl.when(step + 1 < total)
    def _():
        nxt = step + 1
        for cp in copies(nxt // nchunks, lax.rem(nxt, nchunks), 1 - slot):
            cp.start()

    @pl.when(c == 0)
    def _():
        qe = qexp_ref[0]
        zero = jnp.zeros((t_new, LANES), F32)
        rows_ = []
        for hh in range(nh):
            chunk = qe[:, hh * LANES:(hh + 1) * LANES]
            rows_.append(jnp.concatenate([chunk if tt == hh // 2 else zero for tt in range(npair)], axis=1))
        qbd_ref[...] = jnp.concatenate(rows_, axis=0).astype(BF16)
        m_ref[...] = jnp.full(m_ref.shape, -jnp.inf, F32)
        l_ref[...] = jnp.zeros(l_ref.shape, F32)
        acc_ref[...] = jnp.zeros(acc_ref.shape, F32)

    def update(s, pv_fn):
        m_prev = m_ref[...]
        m_new = jnp.maximum(m_prev, jnp.max(s, axis=1, keepdims=True))
        alpha = jnp.exp2(m_prev - m_new)
        p = jnp.exp2(s - jnp.concatenate([m_new] * (s.shape[1] // LANES), axis=1))
        l_ref[...] = alpha * l_ref[...] + jnp.sum(p, axis=1, keepdims=True)
        acc_ref[...] = jnp.concatenate([alpha] * (aw // LANES), axis=1) * acc_ref[...] + pv_fn(p.astype(BF16))
        m_ref[...] = m_new

    for cp in copies(b, c, slot):
        cp.wait()

    for i in range(0, cpp, 2):
        kb2 = jnp.concatenate([kbuf_ref[slot, i], kbuf_ref[slot, i + 1]], axis=1).astype(BF16)
        vb2 = jnp.concatenate([vbuf_ref[slot, i], vbuf_ref[slot, i + 1]], axis=1).astype(BF16)
        bias2 = jnp.concatenate([bias_ref[0, c * cpp + i], bias_ref[0, c * cpp + i + 1]], axis=1)
        s = jnp.dot(qbd_ref[...], kb2, preferred_element_type=F32) + jnp.concatenate([bias2] * nh, axis=0)
        update(s, lambda pb, vb2=vb2: lax.dot_general(pb, vb2, _NT, preferred_element_type=F32))

    @pl.when(c == nchunks - 1)
    def _():
        pad = jnp.zeros((LANES - t_new, aw), F32)
        knp = jnp.concatenate([kn_ref[0], pad], axis=0).astype(BF16)
        vnp = jnp.concatenate([vn_ref[0], pad], axis=0).astype(BF16)
        s = (lax.dot_general(qbd_ref[...], knp, _NT, preferred_element_type=F32)
             + jnp.concatenate([bias_ref[0, nchunks * cpp]] * nh, axis=0))
        update(s, lambda pb: jnp.dot(pb, vnp, preferred_element_type=F32))
        low = _lane_low((1, LANES))
        acc = acc_ref[...]
        inv = 1.0 / l_ref[...]
        outs = []
        for pp in range(npair):
            r0 = slice(2 * pp * t_new, (2 * pp + 1) * t_new)
            r1 = slice((2 * pp + 1) * t_new, (2 * pp + 2) * t_new)
            cs = slice(pp * LANES, (pp + 1) * LANES)
            outs.append(jnp.where(low, acc[r0, cs] * inv[r0], acc[r1, cs] * inv[r1]))
        o_ref[0] = jnp.concatenate(outs, axis=1)


def _sattn(s, bias, k_t, v_t, page_table, *, cpp):
    nb, t_new, aw = s["k"].shape
    nh = aw // HEAD_DIM
    npages = page_table.shape[1]
    nchunks = npages // cpp
    seq_blk = lambda c: pl.BlockSpec((1, t_new, c), lambda b, ch, pt: (b, 0, 0))
    grid_spec = pltpu.PrefetchScalarGridSpec(
        num_scalar_prefetch=1, grid=(nb, nchunks),
        in_specs=[seq_blk(nh * LANES),
                  pl.BlockSpec((1, npages + 1, t_new, LANES), lambda b, ch, pt: (b, 0, 0, 0)),
                  seq_blk(aw), seq_blk(aw),
                  pl.BlockSpec(memory_space=pl.ANY), pl.BlockSpec(memory_space=pl.ANY)],
        out_specs=seq_blk(aw),
        scratch_shapes=[pltpu.VMEM((2, cpp, aw, PAGE_SIZE), F32),
                        pltpu.VMEM((2, cpp, aw, PAGE_SIZE), F32),
                        pltpu.SemaphoreType.DMA((2, 2)),
                        pltpu.VMEM((nh * t_new, aw), BF16),
                        pltpu.VMEM((nh * t_new, LANES), F32),
                        pltpu.VMEM((nh * t_new, LANES), F32),
                        pltpu.VMEM((nh * t_new, aw), F32)])
    return pl.pallas_call(
        functools.partial(_sattn_kernel, cpp=cpp, nchunks=nchunks),
        grid_spec=grid_spec,
        out_shape=jax.ShapeDtypeStruct((nb, t_new, aw), F32),
        compiler_params=_params(2),
        name="sattn",
    )(page_table, s["qexp"], bias, s["k"], s["v"], k_t, v_t)


def _ffn_kernel(x_ref, mod_ref, attn_ref, bconv_ref, wot_ref, wob_ref, g2_ref, wg_ref, wu_ref, wfc_ref, bfc_ref,
                wd_ref, prev_ref, y_ref, fst_ref, gbuf_ref, *, carry, fchunk):
    nseq, rows, d = x_ref.shape
    m = nseq * rows
    f = wg_ref.shape[1]
    x = x_ref[...]
    mod = mod_ref[...]
    gt1 = mod[:, :, 2 * d:3 * d]
    gt2 = mod[:, :, 5 * d:6 * d]
    a = attn_ref[...].reshape(m, attn_ref.shape[-1]).astype(BF16)
    bc = bconv_ref[...].reshape(m, bconv_ref.shape[-1]).astype(BF16)
    mix = (jnp.dot(a, wot_ref[...], preferred_element_type=F32)
           + jnp.dot(bc, wob_ref[...], preferred_element_type=F32))
    x1 = x + gt1 * mix.reshape(nseq, rows, d)
    hb = _modulated_norm(x1, mod[:, :, 3 * d:5 * d], g2_ref[...], d).reshape(m, d).astype(BF16)

    if carry:
        i = pl.program_id(1)

        @pl.when(i == 0)
        def _():
            gbuf_ref[:, 6:8, :] = jnp.zeros((nseq, 2, f), F32)

        @pl.when(i > 0)
        def _():
            gbuf_ref[:, 6:8, :] = gbuf_ref[:, rows + 6:rows + 8, :]
    else:
        gbuf_ref[:, 6:8, :] = prev_ref[...]

    acc = jnp.zeros((m, d), F32)
    for fc in range(f // fchunk):
        sl = slice(fc * fchunk, (fc + 1) * fchunk)
        g = jnp.dot(hb, wg_ref[:, sl], preferred_element_type=F32).reshape(nseq, rows, fchunk)
        u = jnp.dot(hb, wu_ref[:, sl], preferred_element_type=F32).reshape(nseq, rows, fchunk)
        gbuf_ref[:, 8:, sl] = g
        wfc = wfc_ref[:, sl]
        gc = wfc[0:1, :][None] * gbuf_ref[:, 6:6 + rows, sl]
        gc = gc + wfc[1:2, :][None] * gbuf_ref[:, 7:7 + rows, sl]
        gc = gc + wfc[2:3, :][None] * g
        z = gc + bfc_ref[:, sl][None]
        act = (z * jax.nn.sigmoid(z)) * u
        acc = acc + jnp.dot(act.reshape(m, fchunk).astype(BF16), wd_ref[sl, :], preferred_element_type=F32)
        fst_ref[:, :, sl] = g[:, rows - 2:rows, :]
    y_ref[...] = x1 + gt2 * acc.reshape(nseq, rows, d)


def _ffn(x, mod, attn, bconv, prev, wts, *, carry, rows, nseq, fchunk):
    nb, t, d = x.shape
    f = wts["wg"].shape[1]
    if carry:
        grid = (nb, t // rows)
        tile = lambda c: pl.BlockSpec((1, rows, c), lambda b, i: (b, i, 0))
        per_seq = lambda r, c: pl.BlockSpec((1, r, c), lambda b, i: (b, 0, 0))
    else:
        grid = (1, nb // nseq)
        tile = lambda c: pl.BlockSpec((nseq, rows, c), lambda b, i: (i, 0, 0))
        per_seq = lambda r, c: pl.BlockSpec((nseq, r, c), lambda b, i: (i, 0, 0))
    in_specs = [tile(d), per_seq(1, mod.shape[-1]), tile(attn.shape[-1]), tile(bconv.shape[-1]),
                _const_spec(wts["wot"].shape), _const_spec(wts["wob"].shape), _const_spec((1, d)),
                _const_spec(wts["wg"].shape), _const_spec(wts["wu"].shape), _const_spec((CONV_K, f)),
                _const_spec((1, f)), _const_spec(wts["wd"].shape), per_seq(CONV_K - 1, f)]
    return pl.pallas_call(
        functools.partial(_ffn_kernel, carry=carry, fchunk=fchunk),
        grid=grid, in_specs=in_specs,
        out_specs=[tile(d), per_seq(CONV_K - 1, f)],
        out_shape=[jax.ShapeDtypeStruct((nb, t, d), F32), jax.ShapeDtypeStruct((nb, CONV_K - 1, f), F32)],
        scratch_shapes=[pltpu.VMEM((nseq, rows + 8, f), F32)],
        compiler_params=_params(2),
        name="ffn_prompt" if carry else "ffn_sample",
    )(x, mod, attn, bconv, wts["wot"], wts["wob"], wts["g2"], wts["wg"], wts["wu"], wts["wfc"], wts["bfc"],
      wts["wd"], prev)


def _rope_angles(pos):
    inv = ROPE_THETA ** (-jnp.arange(0, ROPE_DIM, 2, dtype=F32) / ROPE_DIM)
    ang = pos.astype(F32)[:, None] * inv[None, :]
    return jnp.cos(ang), jnp.sin(ang)


def _rope_tables(pos):
    cos, sin = _rope_angles(pos)
    t = pos.shape[0]
    pad1 = jnp.ones((t, HEAD_DIM - ROPE_DIM), F32)
    pad0 = jnp.zeros((t, HEAD_DIM - ROPE_DIM), F32)
    zh = jnp.zeros((t, ROPE_HALF), F32)
    c64 = jnp.concatenate([cos, cos, pad1], axis=1)
    a64 = jnp.concatenate([-sin, zh, pad0], axis=1)
    b64 = jnp.concatenate([zh, sin, pad0], axis=1)
    rep = LANES // HEAD_DIM
    return tuple(jnp.tile(x, (1, rep)) for x in (c64, a64, b64))


def _layer_weights(l, d, g_mix_norm, w_in, g_q, g_k, w_short_conv, w_o, g_ffn_norm, w_gate, w_up, w_ffn_conv,
                   b_ffn_conv, w_down):
    aw = d // 2
    cw = d - aw
    nh = aw // HEAD_DIM
    w = w_in[l]
    nqi = N_IDX_HEADS * IDX_DIM
    o_qi = 3 * aw
    o_ki = o_qi + nqi
    o_wi = o_ki + IDX_DIM
    o_cv = o_wi + N_IDX_HEADS
    widx = jnp.concatenate([w[:, o_qi:o_ki], w[:, o_wi:o_cv],
                            jnp.zeros((d, IDX_PAD - nqi - N_IDX_HEADS), F32)], axis=1)
    wih, wil = _split_bf16(widx)
    wki = w[:, o_ki:o_wi]
    wkih, wkil = _split_bf16(wki)
    pad_ki = lambda a: jnp.pad(a, ((0, 0), (0, LANES - IDX_DIM)))
    head = jnp.arange(aw) // HEAD_DIM
    bd = jnp.where(head[:, None] == head[None, :], 1.0 / HEAD_DIM, 0.0).astype(BF16)
    return dict(
        gmix=g_mix_norm[l].reshape(1, d),
        wq=w[:, 0:aw].astype(BF16), wih=wih, wil=wil,
        wcv=w[:, o_cv:o_cv + 3 * cw].astype(BF16), bd=bd,
        wkv=jnp.concatenate([w[:, aw:3 * aw].astype(BF16), wkih], axis=1).T, wkil=wkil.T,
        wkvs=w[:, aw:3 * aw].astype(BF16), wkih=pad_ki(wkih), wkils=pad_ki(wkil),
        gq=jnp.tile(g_q[l], nh).reshape(1, aw), gk=jnp.tile(g_k[l], nh).reshape(1, aw),
        gkc=g_k[l].reshape(HEAD_DIM, 1),
        wsc=w_short_conv[l],
        wot=w_o[l][0:aw].astype(BF16), wob=w_o[l][aw:].astype(BF16),
        g2=g_ffn_norm[l].reshape(1, d),
        wg=w_gate[l].astype(BF16), wu=w_up[l].astype(BF16), wfc=w_ffn_conv[l],
        bfc=b_ffn_conv[l].reshape(1, -1), wd=w_down[l].astype(BF16))


def _pick(n, candidates):
    for c in candidates:
        if n % c == 0:
            return c
    raise ValueError(f"no tile size in {candidates} divides {n}")


def kernel(x_prompt, x_sample, cache_k, cache_v, cache_kidx, state_conv, state_ffn_conv, page_table, c_prompt,
           c_sample, w_ada, b_ada, g_mix_norm, w_in, g_q, g_k, w_short_conv, w_o, g_ffn_norm, w_gate, w_up,
           w_ffn_conv, b_ffn_conv, w_down):
    nbp, seq, d = x_prompt.shape
    nbs, t_new, _ = x_sample.shape
    depth = w_in.shape[0]
    aw = d // 2
    cw = d - aw
    nh = aw // HEAD_DIM
    f = w_gate.shape[-1]
    npages = page_table.shape[1]
    past = npages * PAGE_SIZE
    n_pool = cache_k.shape[1]
    assert t_new == SUBLANES and cache_k.shape[2] == PAGE_SIZE and aw % LANES == 0 and nh % 2 == 0

    blk = _pick(seq, (256, 128))
    rows_in = _pick(seq, (512, 256, 128))
    rows_ffn = _pick(seq, (256, 128))
    nseq = _pick(nbs, (32, 16, 8, 4, 2, 1))
    cpp = _pick(npages, (8, 4, 2))
    fchunk = f // 2 if (f // 2) % LANES == 0 else f
    top_p = min(TOPK_MAX, seq // TOPK_FRAC)
    top_s = min(TOPK_MAX, (past + t_new) // TOPK_FRAC)

    pos_p = jnp.arange(seq)
    tabs_p = _rope_tables(pos_p)
    tabs_pt = tuple(a.T for a in _rope_angles(pos_p))
    tabs_s = _rope_tables(past + jnp.arange(t_new))
    zeros_ffn = jnp.zeros((nbp, CONV_K - 1, f), F32)

    xp, xs = x_prompt, x_sample
    outs = {k: [] for k in ("kp", "vp", "kip", "cp", "fp", "ks", "vs", "kis", "cs", "fs")}
    for l in range(depth):
        wts = _layer_weights(l, d, g_mix_norm, w_in, g_q, g_k, w_short_conv, w_o, g_ffn_norm, w_gate, w_up,
                             w_ffn_conv, b_ffn_conv, w_down)
        mod = _ada(jnp.concatenate([c_prompt, c_sample], axis=0), w_ada[l], b_ada[l])
        mod_p = mod[:nbp].reshape(nbp, 1, 6 * d)
        mod_s = mod[nbp:].reshape(nbs, 1, 6 * d)

        p = _inproj_prompt(xp, mod_p, wts, tabs_p, tabs_pt, rows=rows_in, blk=blk)
        attn_p = _pattn(p, top=top_p, blk=blk)
        xp, fst_p = _ffn(xp, mod_p, attn_p, p["bconv"], zeros_ffn, wts, carry=True, rows=rows_ffn, nseq=1,
                         fchunk=fchunk)

        k_t = cache_k[l].transpose(0, 2, 3, 1).reshape(n_pool, aw, PAGE_SIZE)
        v_t = cache_v[l].transpose(0, 2, 3, 1).reshape(n_pool, aw, PAGE_SIZE)
        kidx_t = cache_kidx[l].transpose(0, 2, 1)
        s = _inproj_sample(xs, mod_s, state_conv[l], wts, tabs_s, nseq=nseq)
        keys = _sidx(s, kidx_t, page_table)
        bias = _ssel(keys, top=top_s, nseq=nseq)
        attn_s = _sattn(s, bias, k_t, v_t, page_table, cpp=cpp)
        xs, fst_s = _ffn(xs, mod_s, attn_s, s["bconv"], state_ffn_conv[l], wts, carry=False, rows=t_new, nseq=nseq,
                         fchunk=fchunk)

        outs["kp"].append(p["kt"].reshape(nbp, nh, HEAD_DIM, seq).transpose(0, 3, 1, 2))
        outs["vp"].append(p["vt"].reshape(nbp, nh, HEAD_DIM, seq).transpose(0, 3, 1, 2))
        outs["kip"].append(p["kit"].transpose(0, 2, 1))
        outs["cp"].append(p["cstate"])
        outs["fp"].append(fst_p)
        outs["ks"].append(s["k"].reshape(nbs, t_new, nh, HEAD_DIM))
        outs["vs"].append(s["v"].reshape(nbs, t_new, nh, HEAD_DIM))
        outs["kis"].append(s["kidx"])
        outs["cs"].append(s["cstate"])
        outs["fs"].append(fst_s)

    st = lambda k: jnp.stack(outs[k])
    return (xp, xs, st("kp"), st("vp"), st("kip"), st("cp"), st("fp"),
            st("ks"), st("vs"), st("kis"), st("cs"), st("fs"))
```

```python
import functools
import math

import jax
import jax.numpy as jnp
from jax import lax
from jax.experimental import pallas as pl
from jax.experimental.pallas import tpu as pltpu

F32 = jnp.float32
BF16 = jnp.bfloat16
I32 = jnp.int32

HEAD_DIM = 64
IDX_DIM = 64
N_IDX_HEADS = 4
ROPE_DIM = 16
ROPE_HALF = ROPE_DIM // 2
ROPE_THETA = 500000.0
CONV_K = 3
TOPK_MAX = 256
TOPK_FRAC = 4
PAGE_SIZE = 128
EPS = 1e-6

LANES = 128
SUBLANES = 8
IDX_PAD = 384
QI3 = 256
NEG_BIAS = -1e30
INT_MIN = -(2 ** 31)
VMEM_LIMIT = 56 * 1024 * 1024
LOG2E = math.log2(math.e)

_NT = (((1,), (1,)), ((), ()))


def _sortable(x):
    b = lax.bitcast_convert_type(x, I32)
    return b ^ (lax.shift_right_arithmetic(b, 31) & 0x7FFFFFFF)


_NEG_INF_KEY = (0xFF800000 ^ 0x7FFFFFFF) - 2 ** 32


def _lane_low(shape):
    return (lax.broadcasted_iota(I32, shape, len(shape) - 1) % LANES) < (LANES // 2)


def _split_bf16(x):
    hi = x.astype(BF16)
    lo = (x - hi.astype(F32)).astype(BF16)
    return hi, lo


MAX_PREDS = 2


def _one_if(cond):
    return jnp.where(cond, 1.0, 0.0)


def _bit_search(count, nbits, start, ones, pred_of, accept):
    def step(i, t):
        cand = t | jnp.left_shift(ones, nbits - 1 - i)
        (n,) = count([pred_of(cand)])
        return jnp.where(accept(n), cand, t)

    return lax.fori_loop(0, nbits, step, start)


def _select_params(count, top, nbits_idx, ones):
    kf = float(top)
    zero = ones * 0
    (c0,) = count([lambda kk, idx, rs: _one_if(kk >= 0)])
    t0 = jnp.where(c0 >= kf, zero, zero + INT_MIN)
    t = _bit_search(count, 31, t0, ones, lambda cand: (lambda kk, idx, rs: _one_if(kk >= cand[rs])),
                    lambda n: n >= kf)
    cgt, cge = count([lambda kk, idx, rs: _one_if(kk > t[rs]), lambda kk, idx, rs: _one_if(kk >= t[rs])])
    r = kf - cgt

    def jsearch():
        return _bit_search(
            count, nbits_idx, zero, ones,
            lambda cand: (lambda kk, idx, rs: jnp.where(kk == t[rs], _one_if(idx < cand[rs]), 0.0)),
            lambda n: n < r)

    need = jnp.max(cge) > kf
    j = lax.cond(need, jsearch, lambda: zero + (2 ** nbits_idx))
    return t, j


def _bias_tile(kk, idx, t, j):
    tie = jnp.where(kk == t, jnp.where(idx <= j, 0.0, NEG_BIAS), NEG_BIAS)
    bias = jnp.where(kk > t, 0.0, tie)
    return jnp.where(kk == _NEG_INF_KEY, NEG_BIAS, bias)


def _const_spec(shape):
    nd = len(shape)
    return pl.BlockSpec(shape, lambda *_: (0,) * nd, pipeline_mode=pl.Buffered(1))


def _params(n_grid):
    return pltpu.CompilerParams(dimension_semantics=("arbitrary",) * n_grid, vmem_limit_bytes=VMEM_LIMIT)


def _ada_kernel(c_ref, w_ref, b_ref, o_ref):
    c = c_ref[...]
    a = c * jax.nn.sigmoid(c)
    o_ref[...] = jnp.dot(a, w_ref[...], preferred_element_type=F32,
                         precision=lax.Precision.HIGHEST) + b_ref[...]


def _ada(c_all, w_ada, b_ada):
    n, d = c_all.shape
    n6 = w_ada.shape[1]
    return pl.pallas_call(
        _ada_kernel,
        grid=(n6 // d,),
        in_specs=[pl.BlockSpec((n, d), lambda j: (0, 0)),
                  pl.BlockSpec((d, d), lambda j: (0, j)),
                  pl.BlockSpec((1, d), lambda j: (0, j))],
        out_specs=pl.BlockSpec((n, d), lambda j: (0, j)),
        out_shape=jax.ShapeDtypeStruct((n, n6), F32),
        compiler_params=_params(1),
        name="ada",
    )(c_all, w_ada, b_ada.reshape(1, n6))


def _modulated_norm(x, mod, g, d):
    ms = jnp.mean(x * x, axis=-1, keepdims=True)
    h = (x * lax.rsqrt(ms + EPS)) * g[None]
    return h * (1.0 + mod[:, :, d:2 * d]) + mod[:, :, 0:d]


def _rope_lanes(x2, nseq, rows, cos, sa, sb):
    m, w = x2.shape
    xm = pltpu.roll(x2, w - ROPE_HALF, 1)
    xp = pltpu.roll(x2, ROPE_HALF, 1)
    y = (x2.reshape(nseq, rows, w) * cos[None] + xm.reshape(nseq, rows, w) * sa[None]
         + xp.reshape(nseq, rows, w) * sb[None])
    return y.reshape(m, w)


def _rope_rows(x3, cos, sin):
    x1 = x3[:, 0:ROPE_HALF, :]
    x2 = x3[:, ROPE_HALF:ROPE_DIM, :]
    return jnp.concatenate([x1 * cos[None] - x2 * sin[None], x2 * cos[None] + x1 * sin[None],
                            x3[:, ROPE_DIM:, :]], axis=1)


def _qexp_output(q, nseq, rows, tabs, bd_ref, gq, qexp_ref):
    m, aw = q.shape
    nh = aw // HEAD_DIM
    cos, sa, sb = tabs
    reps = aw // LANES
    ss = jnp.dot((q * q).astype(BF16), bd_ref[...], preferred_element_type=F32)
    q = (q * lax.rsqrt(ss + EPS)) * gq
    q = _rope_lanes(q, nseq, rows, jnp.concatenate([cos] * reps, axis=1), jnp.concatenate([sa] * reps, axis=1),
                    jnp.concatenate([sb] * reps, axis=1))
    low = _lane_low((1, LANES))
    qs = q * (HEAD_DIM ** -0.5 * LOG2E)
    pieces = []
    for hh in range(nh):
        chunk = qs[:, (hh // 2) * LANES:(hh // 2 + 1) * LANES]
        keep = low if hh % 2 == 0 else jnp.logical_not(low)
        pieces.append(jnp.where(keep, chunk, 0.0))
    qexp_ref[...] = jnp.concatenate(pieces, axis=1).reshape(nseq, rows, nh * LANES).astype(qexp_ref.dtype)


def _q_outputs(q, pidx, nseq, rows, tabs, bd_ref, gq, qexp_ref, qi3_ref, wi_ref):
    m = q.shape[0]
    cos, sa, sb = tabs
    low = _lane_low((1, LANES))
    _qexp_output(q, nseq, rows, tabs, bd_ref, gq, qexp_ref)

    nqi = N_IDX_HEADS * IDX_DIM
    qi = _rope_lanes(pidx[:, 0:nqi], nseq, rows, jnp.concatenate([cos] * (nqi // LANES), axis=1),
                     jnp.concatenate([sa] * (nqi // LANES), axis=1), jnp.concatenate([sb] * (nqi // LANES), axis=1))
    wi_ref[...] = (pidx[:, nqi:nqi + LANES] * ((IDX_DIM ** -0.5) * (N_IDX_HEADS ** -0.5))).reshape(nseq, rows, LANES)
    zero = jnp.zeros((m, LANES), F32)
    q3 = []
    for t in range(nqi // LANES):
        xt = qi[:, t * LANES:(t + 1) * LANES]
        hi = xt.astype(BF16).astype(F32)
        lo = xt - hi
        hi_r = pltpu.roll(hi, LANES // 2, 1)
        lo_r = pltpu.roll(lo, LANES // 2, 1)
        q3 += [jnp.where(low, hi, hi_r), jnp.where(low, lo, zero),
               jnp.where(low, hi_r, hi), jnp.where(low, lo_r, zero)]
    qi3_ref[...] = jnp.concatenate(q3, axis=1).reshape(nseq, rows, N_IDX_HEADS * QI3).astype(qi3_ref.dtype)


def _conv_branch(pcv, nseq, rows, cw, wsc_ref, cbuf_ref, bconv_ref, cst_ref):
    bg = pcv[:, 0:cw].reshape(nseq, rows, cw)
    cu = (pcv[:, cw:2 * cw] * pcv[:, 2 * cw:3 * cw]).reshape(nseq, rows, cw)
    cbuf_ref[:, 8:, :] = cu
    wsc = wsc_ref[...]
    conv = wsc[0:1, :][None] * cbuf_ref[:, 6:6 + rows, :]
    conv = conv + wsc[1:2, :][None] * cbuf_ref[:, 7:7 + rows, :]
    conv = conv + wsc[2:3, :][None] * cu
    bconv_ref[...] = (bg * conv).astype(bconv_ref.dtype)
    cst_ref[...] = cu[:, rows - 2:rows, :]


def _inproj_p_kernel(x_ref, mod_ref, gmix_ref, wq_ref, wfm_ref, wfml_ref, wkih_ref, wkil_ref, wcv_ref, bd_ref,
                     gq_ref, gkc_ref, cos_ref, sa_ref, sb_ref, cost_ref, sint_ref, wsc_ref,
                     qexp_ref, qi3t_ref, wit_ref, ki3_ref, bconv_ref, cst_ref, kt_ref, vt_ref, kit_ref, ktb_ref,
                     vta_ref, cbuf_ref):
    _, rows, d = x_ref.shape
    aw = kt_ref.shape[1]
    cw = bconv_ref.shape[-1]
    nh = aw // HEAD_DIM
    nblk, _, blk = ktb_ref.shape[1:]
    nqi = N_IDX_HEADS * IDX_DIM

    h2 = _modulated_norm(x_ref[...], mod_ref[...], gmix_ref[...], d).reshape(rows, d)
    hb, hl = _split_bf16(h2)
    cos, sa, sb = cos_ref[...], sa_ref[...], sb_ref[...]
    cost = cost_ref[...]
    sint = sint_ref[...]

    _qexp_output(jnp.dot(hb, wq_ref[...], preferred_element_type=F32), 1, rows, (cos, sa, sb), bd_ref, gq_ref[...],
                 qexp_ref)

    fm = lax.dot_general(wfm_ref[...], hb, _NT, preferred_element_type=F32)
    o_i = 2 * aw
    idx = (fm[o_i:] + lax.dot_general(wfm_ref[o_i:, :], hl, _NT, preferred_element_type=F32)
           + lax.dot_general(wfml_ref[...], hb, _NT, preferred_element_type=F32))

    k3 = fm[0:aw].reshape(nh, HEAD_DIM, rows)
    ss = jnp.mean(k3 * k3, axis=1, keepdims=True)
    k3 = _rope_rows((k3 * lax.rsqrt(ss + EPS)) * gkc_ref[...][None], cost, sint)
    kt = k3.reshape(aw, rows)
    vt = fm[aw:2 * aw]
    qi = _rope_rows(idx[0:nqi].reshape(N_IDX_HEADS, IDX_DIM, rows), cost, sint)
    kit = _rope_rows(idx[nqi:nqi + IDX_DIM].reshape(1, IDX_DIM, rows), cost, sint).reshape(IDX_DIM, rows)
    kt_ref[0] = kt
    vt_ref[0] = vt
    kit_ref[0] = kit
    wit_ref[0] = idx[nqi + IDX_DIM:nqi + IDX_DIM + SUBLANES] * ((IDX_DIM ** -0.5) * (N_IDX_HEADS ** -0.5))

    zeros = jnp.zeros((HEAD_DIM, rows), BF16)
    ones = jnp.ones((HEAD_DIM, rows), BF16)
    qih = qi.astype(BF16)
    qil = (qi - qih.astype(F32)).astype(BF16)
    qi3t_ref[0] = jnp.concatenate(
        [b_ for hh in range(N_IDX_HEADS) for b_ in (qih[hh], qih[hh], qil[hh], zeros)], axis=0)
    ktb = kt.astype(BF16)
    vtb = vt.astype(BF16)
    vta = jnp.concatenate(
        [b_ for hh in range(nh) for b_ in
         ((vtb[hh * HEAD_DIM:(hh + 1) * HEAD_DIM], ones) if hh % 2 == 0 else
          (ones, vtb[hh * HEAD_DIM:(hh + 1) * HEAD_DIM]))], axis=0)
    for bb in range(nblk):
        cs = slice(bb * blk, (bb + 1) * blk)
        ktb_ref[0, bb] = ktb[:, cs]
        vta_ref[0, bb] = vta[:, cs]

    ki = (jnp.dot(hb, wkih_ref[...], preferred_element_type=F32)
          + jnp.dot(hl, wkih_ref[...], preferred_element_type=F32)
          + jnp.dot(hb, wkil_ref[...], preferred_element_type=F32))
    low = _lane_low((1, LANES))
    ki = _rope_lanes(ki, 1, rows, jnp.where(low, cos, 1.0), jnp.where(low, sa, 0.0), jnp.where(low, sb, 0.0))
    khi = ki.astype(BF16).astype(F32)
    klo = ki - khi
    zero = jnp.zeros((rows, LANES), F32)
    ki3_ref[0] = jnp.concatenate([jnp.where(low, khi, pltpu.roll(klo, LANES // 2, 1)), jnp.where(low, khi, zero)],
                                 axis=1).astype(BF16)

    i = pl.program_id(1)

    @pl.when(i == 0)
    def _():
        cbuf_ref[:, 6:8, :] = jnp.zeros((1, 2, cw), F32)

    @pl.when(i > 0)
    def _():
        cbuf_ref[:, 6:8, :] = cbuf_ref[:, rows + 6:rows + 8, :]

    _conv_branch(jnp.dot(hb, wcv_ref[...], preferred_element_type=F32), 1, rows, cw, wsc_ref, cbuf_ref,
                 bconv_ref, cst_ref)


def _inproj_prompt(x, mod, wts, tabs, tabs_t, *, rows, blk):
    nb, t, d = x.shape
    aw = d // 2
    cw = d - aw
    nh = aw // HEAD_DIM
    nblk = rows // blk
    tile = lambda c: pl.BlockSpec((1, rows, c), lambda b, i: (b, i, 0))
    per_seq = lambda r, c: pl.BlockSpec((1, r, c), lambda b, i: (b, 0, 0))
    ftile = lambda r: pl.BlockSpec((1, r, rows), lambda b, i: (b, 0, i))
    btile = lambda r: pl.BlockSpec((1, nblk, r, blk), lambda b, i: (b, i, 0, 0))
    tab = pl.BlockSpec((rows, LANES), lambda b, i: (i, 0))
    tab_t = pl.BlockSpec((ROPE_HALF, rows), lambda b, i: (0, i))
    cos, sa, sb = tabs
    cost, sint = tabs_t
    in_specs = [tile(d), per_seq(1, mod.shape[-1]), _const_spec((1, d)),
                _const_spec(wts["wq"].shape), _const_spec(wts["wfm"].shape), _const_spec(wts["wfml"].shape),
                _const_spec(wts["wkih"].shape), _const_spec(wts["wkils"].shape), _const_spec(wts["wcv"].shape),
                _const_spec(wts["bd"].shape), _const_spec((1, aw)), _const_spec((HEAD_DIM, 1)),
                tab, tab, tab, tab_t, tab_t, _const_spec((CONV_K, cw))]
    out_specs = [tile(nh * LANES), ftile(N_IDX_HEADS * QI3), ftile(SUBLANES), tile(QI3), tile(cw),
                 per_seq(CONV_K - 1, cw), ftile(aw), ftile(aw), ftile(IDX_DIM), btile(aw), btile(nh * LANES)]
    sds = jax.ShapeDtypeStruct
    out_shape = [sds((nb, t, nh * LANES), BF16), sds((nb, N_IDX_HEADS * QI3, t), BF16), sds((nb, SUBLANES, t), F32),
                 sds((nb, t, QI3), BF16), sds((nb, t, cw), BF16), sds((nb, CONV_K - 1, cw), F32),
                 sds((nb, aw, t), F32), sds((nb, aw, t), F32), sds((nb, IDX_DIM, t), F32),
                 sds((nb, t // blk, aw, blk), BF16), sds((nb, t // blk, nh * LANES, blk), BF16)]
    names = ["qexp", "qi3t", "wit", "ki3", "bconv", "cstate", "kt", "vt", "kit", "ktb", "vta"]
    res = pl.pallas_call(
        _inproj_p_kernel,
        grid=(nb, t // rows), in_specs=in_specs, out_specs=out_specs, out_shape=out_shape,
        scratch_shapes=[pltpu.VMEM((1, rows + 8, cw), F32)],
        compiler_params=_params(2),
        name="inproj_prompt",
    )(x, mod, wts["gmix"], wts["wq"], wts["wfm"], wts["wfml"], wts["wkih"], wts["wkils"], wts["wcv"], wts["bd"],
      wts["gq"], wts["gkc"], cos, sa, sb, cost, sint, wts["wsc"])
    return dict(zip(names, res))


def _inproj_s_kernel(x_ref, mod_ref, gmix_ref, wq_ref, wih_ref, wil_ref, wcv_ref, wkvs_ref, wkih_ref, wkil_ref,
                     bd_ref, gq_ref, gk_ref, cos_ref, sa_ref, sb_ref, wsc_ref, prev_ref,
                     qexp_ref, qi3_ref, wi_ref, bconv_ref, cst_ref, k_ref, v_ref, kidx_ref, ki3_ref, cbuf_ref):
    nseq, rows, d = x_ref.shape
    m = nseq * rows
    aw = k_ref.shape[-1]
    cw = bconv_ref.shape[-1]
    h2 = _modulated_norm(x_ref[...], mod_ref[...], gmix_ref[...], d).reshape(m, d)
    hb, hl = _split_bf16(h2)
    cos, sa, sb = cos_ref[...], sa_ref[...], sb_ref[...]

    q = jnp.dot(hb, wq_ref[...], preferred_element_type=F32)
    pidx = (jnp.dot(hb, wih_ref[...], preferred_element_type=F32)
            + jnp.dot(hl, wih_ref[...], preferred_element_type=F32)
            + jnp.dot(hb, wil_ref[...], preferred_element_type=F32))
    _q_outputs(q, pidx, nseq, rows, (cos, sa, sb), bd_ref, gq_ref[...], qexp_ref, qi3_ref, wi_ref)

    kv = jnp.dot(hb, wkvs_ref[...], preferred_element_type=F32)
    k = kv[:, 0:aw]
    ss = jnp.dot((k * k).astype(BF16), bd_ref[...], preferred_element_type=F32)
    k = (k * lax.rsqrt(ss + EPS)) * gk_ref[...]
    reps = aw // LANES
    k = _rope_lanes(k, nseq, rows, jnp.concatenate([cos] * reps, axis=1), jnp.concatenate([sa] * reps, axis=1),
                    jnp.concatenate([sb] * reps, axis=1))
    k_ref[...] = k.reshape(nseq, rows, aw)
    v_ref[...] = kv[:, aw:2 * aw].reshape(nseq, rows, aw)

    ki = (jnp.dot(hb, wkih_ref[...], preferred_element_type=F32)
          + jnp.dot(hl, wkih_ref[...], preferred_element_type=F32)
          + jnp.dot(hb, wkil_ref[...], preferred_element_type=F32))
    low = _lane_low((1, LANES))
    ki = _rope_lanes(ki, nseq, rows, jnp.where(low, cos, 1.0), jnp.where(low, sa, 0.0), jnp.where(low, sb, 0.0))
    kidx_ref[...] = ki[:, 0:IDX_DIM].reshape(nseq, rows, IDX_DIM)
    khi = ki.astype(BF16).astype(F32)
    klo = ki - khi
    zero = jnp.zeros((m, LANES), F32)
    k3 = jnp.concatenate([jnp.where(low, khi, pltpu.roll(klo, LANES // 2, 1)), jnp.where(low, khi, zero)], axis=1)
    ki3_ref[...] = k3.reshape(nseq, rows, QI3)

    cbuf_ref[:, 6:8, :] = prev_ref[...]
    _conv_branch(jnp.dot(hb, wcv_ref[...], preferred_element_type=F32), nseq, rows, cw, wsc_ref, cbuf_ref,
                 bconv_ref, cst_ref)


def _inproj_sample(x, mod, prev, wts, tabs, *, nseq):
    nb, rows, d = x.shape
    aw = d // 2
    cw = d - aw
    nh = aw // HEAD_DIM
    tile = lambda c: pl.BlockSpec((nseq, rows, c), lambda i: (i, 0, 0))
    per_seq = lambda r, c: pl.BlockSpec((nseq, r, c), lambda i: (i, 0, 0))
    tab = pl.BlockSpec((rows, LANES), lambda i: (0, 0))
    cos, sa, sb = tabs
    in_specs = [tile(d), per_seq(1, mod.shape[-1]), _const_spec((1, d)),
                _const_spec(wts["wq"].shape), _const_spec(wts["wih"].shape), _const_spec(wts["wil"].shape),
                _const_spec(wts["wcv"].shape), _const_spec(wts["wkvs"].shape), _const_spec(wts["wkih"].shape),
                _const_spec(wts["wkils"].shape), _const_spec(wts["bd"].shape), _const_spec((1, aw)),
                _const_spec((1, aw)), tab, tab, tab, _const_spec((CONV_K, cw)), per_seq(CONV_K - 1, cw)]
    cols = [("qexp", nh * LANES), ("qi3", N_IDX_HEADS * QI3), ("wi", LANES), ("bconv", cw)]
    cols2 = [("k", aw), ("v", aw), ("kidx", IDX_DIM), ("ki3", QI3)]
    out_specs = ([tile(c) for _, c in cols] + [per_seq(CONV_K - 1, cw)] + [tile(c) for _, c in cols2])
    sds = jax.ShapeDtypeStruct
    out_shape = ([sds((nb, rows, c), F32) for _, c in cols] + [sds((nb, CONV_K - 1, cw), F32)]
                 + [sds((nb, rows, c), F32) for _, c in cols2])
    names = [n for n, _ in cols] + ["cstate"] + [n for n, _ in cols2]
    res = pl.pallas_call(
        _inproj_s_kernel,
        grid=(nb // nseq,), in_specs=in_specs, out_specs=out_specs, out_shape=out_shape,
        scratch_shapes=[pltpu.VMEM((nseq, rows + 8, cw), F32)],
        compiler_params=_params(1),
        name="inproj_sample",
    )(x, mod, wts["gmix"], wts["wq"], wts["wih"], wts["wil"], wts["wcv"], wts["wkvs"], wts["wkih"], wts["wkils"],
      wts["bd"], wts["gq"], wts["gk"], cos, sa, sb, wts["wsc"], prev)
    return dict(zip(names, res))


def _pattn_kernel(qexp_ref, qi3t_ref, wit_ref, ki3_ref, ktb_ref, vta_ref, o_ref, keys_ref, m_ref, acc_ref,
                  *, top, blk, seq):
    nh = acc_ref.shape[0]
    nlt = blk // LANES
    nsl = blk // SUBLANES
    qb = pl.program_id(1)
    nkb = qb + 1
    q0 = qb * blk
    wit = wit_ref[0]
    rowk = lax.broadcasted_iota(I32, (blk, blk), 0)
    colq = lax.broadcasted_iota(I32, (blk, blk), 1)
    sub = lax.broadcasted_iota(I32, (SUBLANES, blk), 0)
    ones = jnp.ones((SUBLANES, blk), I32)
    every = slice(None)

    def score_body(kb, carry_):
        off = pl.multiple_of(kb * blk, blk)
        kblk = ki3_ref[0, pl.ds(off, blk), :]
        acc = jnp.zeros((blk, blk), F32)
        for hh in range(N_IDX_HEADS):
            dd = jnp.dot(kblk, qi3t_ref[0, hh * QI3:(hh + 1) * QI3, :], preferred_element_type=F32)
            acc = acc + jnp.maximum(dd, 0.0) * wit[hh:hh + 1, :]
        acc = jnp.where(acc == 0.0, 0.0, acc)
        acc = jnp.where(kb * blk + rowk <= q0 + colq, acc, -jnp.inf)
        keys_ref[kb] = _sortable(acc)
        return carry_

    lax.fori_loop(0, nkb, score_body, 0)

    nchain = 4

    def count(preds):
        def body(kb, cs):
            cs = [list(c) for c in cs]
            for g in range(nsl):
                kk = keys_ref[kb, g * SUBLANES:(g + 1) * SUBLANES, :]
                idx = kb * blk + g * SUBLANES + sub
                for a, pred in enumerate(preds):
                    cs[a][g % nchain] = cs[a][g % nchain] + pred(kk, idx, every)
            return tuple(tuple(c) for c in cs)

        zero_c = tuple(jnp.zeros((SUBLANES, blk), F32) for _ in range(nchain))
        cs = lax.fori_loop(0, nkb, body, tuple(zero_c for _ in preds))
        return [jnp.sum(sum(c[1:], c[0]), axis=0, keepdims=True) for c in cs]

    t, j = _select_params(count, top, (seq - 1).bit_length(), ones)

    m_ref[...] = jnp.full(m_ref.shape, -jnp.inf, F32)
    acc_ref[...] = jnp.zeros(acc_ref.shape, F32)
    t_full = jnp.concatenate([t] * nsl, axis=0)
    j_full = jnp.concatenate([j] * nsl, axis=0)

    def attn_body(kb, carry_):
        bias = _bias_tile(keys_ref[kb], kb * blk + rowk, t_full, j_full).T
        for hh in range(nh):
            pp = hh // 2
            qh = qexp_ref[0, :, hh * LANES:(hh + 1) * LANES]
            kh = ktb_ref[0, kb, pp * LANES:(pp + 1) * LANES, :]
            va = vta_ref[0, kb, hh * LANES:(hh + 1) * LANES, :]
            s = jnp.dot(qh, kh, preferred_element_type=F32) + bias
            m_prev = m_ref[hh]
            m_new = jnp.maximum(m_prev, jnp.max(s, axis=1, keepdims=True))
            alpha = jnp.exp2(m_prev - m_new)
            p = jnp.exp2(s - jnp.concatenate([m_new] * nlt, axis=1))
            acc_ref[hh] = alpha * acc_ref[hh] + lax.dot_general(p.astype(BF16), va, _NT, preferred_element_type=F32)
            m_ref[hh] = m_new
        return carry_

    lax.fori_loop(0, nkb, attn_body, 0)

    low = _lane_low((1, LANES))
    outs = []
    for pp in range(nh // 2):
        a0 = acc_ref[2 * pp]
        a1 = acc_ref[2 * pp + 1]
        outs.append(jnp.where(low, a0 / pltpu.roll(a0, LANES // 2, 1), a1 / pltpu.roll(a1, LANES // 2, 1)))
    o_ref[0] = jnp.concatenate(outs, axis=1).astype(o_ref.dtype)


def _pattn(p, *, top, blk):
    nb, nblk, aw, _ = p["ktb"].shape
    seq = nblk * blk
    nh = aw // HEAD_DIM
    qtile = lambda c: pl.BlockSpec((1, blk, c), lambda b, i: (b, i, 0))
    ftile = lambda r: pl.BlockSpec((1, r, blk), lambda b, i: (b, 0, i))
    blocks = lambda r: pl.BlockSpec((1, nblk, r, blk), lambda b, i: (b, 0, 0, 0))
    return pl.pallas_call(
        functools.partial(_pattn_kernel, top=top, blk=blk, seq=seq),
        grid=(nb, nblk),
        in_specs=[qtile(nh * LANES), ftile(N_IDX_HEADS * QI3), ftile(SUBLANES),
                  pl.BlockSpec((1, seq, QI3), lambda b, i: (b, 0, 0)), blocks(aw), blocks(nh * LANES)],
        out_specs=qtile(aw),
        out_shape=jax.ShapeDtypeStruct((nb, seq, aw), BF16),
        scratch_shapes=[pltpu.VMEM((nblk, blk, blk), I32),
                        pltpu.VMEM((nh, blk, LANES), F32),
                        pltpu.VMEM((nh, blk, LANES), F32)],
        compiler_params=_params(2),
        name="pattn",
    )(p["qexp"], p["qi3t"], p["wit"], p["ki3"], p["ktb"], p["vta"])


def _sidx_kernel(pt_ref, qi3_ref, wi_ref, kin_ref, kidx_hbm, keys_ref, buf_ref, sem_ref, *, npages):
    t_new = qi3_ref.shape[1]
    b = pl.program_id(0)
    nb = pl.num_programs(0)
    slot = lax.rem(b, 2)

    def page_copy(seq_i, i, sl):
        return pltpu.make_async_copy(kidx_hbm.at[pt_ref[seq_i, i]], buf_ref.at[sl, i], sem_ref.at[sl])

    @pl.when(b == 0)
    def _():
        for i in range(npages):
            page_copy(0, i, 0).start()

    @pl.when(b + 1 < nb)
    def _():
        for i in range(npages):
            page_copy(b + 1, i, 1 - slot).start()

    qi3 = qi3_ref[0]
    wi = wi_ref[0]
    heads = range(N_IDX_HEADS)
    lhs = jnp.concatenate([qi3[:, hh * QI3:hh * QI3 + LANES] for hh in heads]
                          + [qi3[:, hh * QI3 + LANES:(hh + 1) * QI3] for hh in heads], axis=0).astype(BF16)

    def head_sum(dd):
        acc = jnp.zeros((t_new, LANES), F32)
        for hh in heads:
            acc = acc + jnp.maximum(dd[hh * t_new:(hh + 1) * t_new], 0.0) * wi[:, hh:hh + 1]
        return jnp.where(acc == 0.0, 0.0, acc)

    for i in range(npages):
        page_copy(b, i, slot).wait()

    group = 4 if npages % 4 == 0 else 1

    def page_body(g, carry_):
        for u in range(group):
            i = g * group + u
            kp = buf_ref[slot, i]
            kh, kl = _split_bf16(kp)
            d2 = jnp.dot(lhs, jnp.concatenate([kh, kl], axis=0), preferred_element_type=F32)
            nq = N_IDX_HEADS * t_new
            keys_ref[0, i] = _sortable(head_sum(d2[0:nq] + d2[nq:2 * nq]))
        return carry_

    lax.fori_loop(0, npages // group, page_body, 0)

    q3 = jnp.concatenate([qi3[:, hh * QI3:(hh + 1) * QI3] for hh in heads], axis=0).astype(BF16)
    kn = jnp.concatenate([kin_ref[0], jnp.zeros((LANES - t_new, QI3), F32)], axis=0).astype(BF16)
    sc = head_sum(lax.dot_general(q3, kn, _NT, preferred_element_type=F32))
    rown = lax.broadcasted_iota(I32, (t_new, LANES), 0)
    lanen = lax.broadcasted_iota(I32, (t_new, LANES), 1)
    keys_ref[0, npages] = _sortable(jnp.where(lanen <= rown, sc, -jnp.inf))


def _sidx(s, kidx_t, page_table):
    nb, t_new, _ = s["qi3"].shape
    npages = page_table.shape[1]
    seq_blk = lambda c: pl.BlockSpec((1, t_new, c), lambda b, pt: (b, 0, 0))
    grid_spec = pltpu.PrefetchScalarGridSpec(
        num_scalar_prefetch=1, grid=(nb,),
        in_specs=[seq_blk(N_IDX_HEADS * QI3), seq_blk(LANES), seq_blk(QI3), pl.BlockSpec(memory_space=pl.ANY)],
        out_specs=pl.BlockSpec((1, npages + 1, t_new, LANES), lambda b, pt: (b, 0, 0, 0)),
        scratch_shapes=[pltpu.VMEM((2, npages, IDX_DIM, PAGE_SIZE), F32), pltpu.SemaphoreType.DMA((2,))])
    return pl.pallas_call(
        functools.partial(_sidx_kernel, npages=npages),
        grid_spec=grid_spec,
        out_shape=jax.ShapeDtypeStruct((nb, npages + 1, t_new, LANES), I32),
        compiler_params=_params(1),
        name="sidx",
    )(page_table, s["qi3"], s["wi"], s["ki3"], kidx_t)


def _ssel_kernel(keys_ref, bias_ref, cnt_ref, *, top):
    nseq, np1, t_new, _ = keys_ref.shape
    rows = nseq * t_new
    lane = lax.broadcasted_iota(I32, (rows, LANES), 1)
    ones = jnp.ones((rows, LANES), I32)
    cseq = math.gcd(nseq, 16)
    crow = cseq * t_new
    lane_c = lax.broadcasted_iota(I32, (crow, LANES), 1)

    def count(preds):
        na = len(preds)
        cnt_ref[0:na] = jnp.zeros((na,) + cnt_ref.shape[1:], F32)

        def body(pg, carry_):
            idx = pg * LANES + lane_c
            for ch in range(nseq // cseq):
                rs = slice(ch * crow, (ch + 1) * crow)
                kk = keys_ref[ch * cseq:(ch + 1) * cseq, pg].reshape(crow, LANES)
                for a in range(na):
                    cnt_ref[a, rs, :] = cnt_ref[a, rs, :] + preds[a](kk, idx, rs)
            return carry_

        lax.fori_loop(0, np1, body, 0)
        return [jnp.sum(cnt_ref[a], axis=1, keepdims=True) for a in range(na)]

    t, j = _select_params(count, top, (np1 * LANES - 1).bit_length(), ones)

    def bias_body(pg, carry_):
        bias_ref[:, pg] = _bias_tile(keys_ref[:, pg].reshape(rows, LANES), pg * LANES + lane, t, j).reshape(
            nseq, t_new, LANES)
        return carry_

    lax.fori_loop(0, np1, bias_body, 0)


def _ssel(keys, *, top, nseq):
    nb, np1, t_new, _ = keys.shape
    blk_spec = pl.BlockSpec((nseq, np1, t_new, LANES), lambda i: (i, 0, 0, 0))
    return pl.pallas_call(
        functools.partial(_ssel_kernel, top=top),
        grid=(nb // nseq,), in_specs=[blk_spec], out_specs=blk_spec,
        out_shape=jax.ShapeDtypeStruct(keys.shape, F32),
        scratch_shapes=[pltpu.VMEM((MAX_PREDS, nseq * t_new, LANES), F32)],
        compiler_params=_params(1),
        name="ssel",
    )(keys)


def _sattn_kernel(pt_ref, qexp_ref, bias_ref, kn_ref, vn_ref, k_hbm, v_hbm, o_ref,
                  kbuf_ref, vbuf_ref, sem_ref, qbd_ref, m_ref, l_ref, acc_ref, *, cpp, nchunks):
    t_new = kn_ref.shape[1]
    aw = kn_ref.shape[2]
    nh = aw // HEAD_DIM
    npair = nh // 2
    b = pl.program_id(0)
    c = pl.program_id(1)
    step = b * nchunks + c
    total = pl.num_programs(0) * nchunks
    slot = lax.rem(step, 2)

    def copies(seq_i, ch, sl):
        out = []
        for i in range(cpp):
            pg = pt_ref[seq_i, ch * cpp + i]
            out.append(pltpu.make_async_copy(k_hbm.at[pg], kbuf_ref.at[sl, i], sem_ref.at[0, sl]))
            out.append(pltpu.make_async_copy(v_hbm.at[pg], vbuf_ref.at[sl, i], sem_ref.at[1, sl]))
        return out

    @pl.when(step == 0)
    def _():
        for cp in copies(0, 0, 0):
            cp.start()

    @pl.when(step + 1 < total)
    def _():
        nxt = step + 1
        for cp in copies(nxt // nchunks, lax.rem(nxt, nchunks), 1 - slot):
            cp.start()

    @pl.when(c == 0)
    def _():
        qe = qexp_ref[0]
        zero = jnp.zeros((t_new, LANES), F32)
        rows_ = []
        for hh in range(nh):
            chunk = qe[:, hh * LANES:(hh + 1) * LANES]
            rows_.append(jnp.concatenate([chunk if tt == hh // 2 else zero for tt in range(npair)], axis=1))
        qbd_ref[...] = jnp.concatenate(rows_, axis=0).astype(BF16)
        m_ref[...] = jnp.full(m_ref.shape, -jnp.inf, F32)
        l_ref[...] = jnp.zeros(l_ref.shape, F32)
        acc_ref[...] = jnp.zeros(acc_ref.shape, F32)

    def update(s, pv_fn):
        m_prev = m_ref[...]
        m_new = jnp.maximum(m_prev, jnp.max(s, axis=1, keepdims=True))
        alpha = jnp.exp2(m_prev - m_new)
        p = jnp.exp2(s - jnp.concatenate([m_new] * (s.shape[1] // LANES), axis=1))
        l_ref[...] = alpha * l_ref[...] + jnp.sum(p, axis=1, keepdims=True)
        acc_ref[...] = jnp.concatenate([alpha] * (aw // LANES), axis=1) * acc_ref[...] + pv_fn(p.astype(BF16))
        m_ref[...] = m_new

    for cp in copies(b, c, slot):
        cp.wait()

    kall = jnp.concatenate([kbuf_ref[slot, i] for i in range(cpp)], axis=1).astype(BF16)
    vall = jnp.concatenate([vbuf_ref[slot, i] for i in range(cpp)], axis=1).astype(BF16)
    ball = jnp.concatenate([bias_ref[0, c * cpp + i] for i in range(cpp)], axis=1)
    s_all = jnp.dot(qbd_ref[...], kall, preferred_element_type=F32) + jnp.concatenate([ball] * nh, axis=0)
    update(s_all, lambda pb: lax.dot_general(pb, vall, _NT, preferred_element_type=F32))

    @pl.when(c == nchunks - 1)
    def _():
        pad = jnp.zeros((LANES - t_new, aw), F32)
        knp = jnp.concatenate([kn_ref[0], pad], axis=0).astype(BF16)
        vnp = jnp.concatenate([vn_ref[0], pad], axis=0).astype(BF16)
        s = (lax.dot_general(qbd_ref[...], knp, _NT, preferred_element_type=F32)
             + jnp.concatenate([bias_ref[0, nchunks * cpp]] * nh, axis=0))
        update(s, lambda pb: jnp.dot(pb, vnp, preferred_element_type=F32))
        low = _lane_low((1, LANES))
        acc = acc_ref[...]
        inv = 1.0 / l_ref[...]
        outs = []
        for pp in range(npair):
            r0 = slice(2 * pp * t_new, (2 * pp + 1) * t_new)
            r1 = slice((2 * pp + 1) * t_new, (2 * pp + 2) * t_new)
            cs = slice(pp * LANES, (pp + 1) * LANES)
            outs.append(jnp.where(low, acc[r0, cs] * inv[r0], acc[r1, cs] * inv[r1]))
        o_ref[0] = jnp.concatenate(outs, axis=1)


def _sattn(s, bias, k_t, v_t, page_table, *, cpp):
    nb, t_new, aw = s["k"].shape
    nh = aw // HEAD_DIM
    npages = page_table.shape[1]
    nchunks = npages // cpp
    seq_blk = lambda c: pl.BlockSpec((1, t_new, c), lambda b, ch, pt: (b, 0, 0))
    grid_spec = pltpu.PrefetchScalarGridSpec(
        num_scalar_prefetch=1, grid=(nb, nchunks),
        in_specs=[seq_blk(nh * LANES),
                  pl.BlockSpec((1, npages + 1, t_new, LANES), lambda b, ch, pt: (b, 0, 0, 0)),
                  seq_blk(aw), seq_blk(aw),
                  pl.BlockSpec(memory_space=pl.ANY), pl.BlockSpec(memory_space=pl.ANY)],
        out_specs=seq_blk(aw),
        scratch_shapes=[pltpu.VMEM((2, cpp, aw, PAGE_SIZE), F32),
                        pltpu.VMEM((2, cpp, aw, PAGE_SIZE), F32),
                        pltpu.SemaphoreType.DMA((2, 2)),
                        pltpu.VMEM((nh * t_new, aw), BF16),
                        pltpu.VMEM((nh * t_new, LANES), F32),
                        pltpu.VMEM((nh * t_new, LANES), F32),
                        pltpu.VMEM((nh * t_new, aw), F32)])
    return pl.pallas_call(
        functools.partial(_sattn_kernel, cpp=cpp, nchunks=nchunks),
        grid_spec=grid_spec,
        out_shape=jax.ShapeDtypeStruct((nb, t_new, aw), F32),
        compiler_params=_params(2),
        name="sattn",
    )(page_table, s["qexp"], bias, s["k"], s["v"], k_t, v_t)


def _ffn_kernel(x_ref, mod_ref, attn_ref, bconv_ref, wot_ref, wob_ref, g2_ref, wg_ref, wu_ref, wfc_ref, bfc_ref,
                wd_ref, prev_ref, y_ref, fst_ref, gbuf_ref, *, carry, fchunk):
    nseq, rows, d = x_ref.shape
    m = nseq * rows
    f = wg_ref.shape[1]
    x = x_ref[...]
    mod = mod_ref[...]
    gt1 = mod[:, :, 2 * d:3 * d]
    gt2 = mod[:, :, 5 * d:6 * d]
    a = attn_ref[...].reshape(m, attn_ref.shape[-1]).astype(BF16)
    bc = bconv_ref[...].reshape(m, bconv_ref.shape[-1]).astype(BF16)
    mix = (jnp.dot(a, wot_ref[...], preferred_element_type=F32)
           + jnp.dot(bc, wob_ref[...], preferred_element_type=F32))
    x1 = x + gt1 * mix.reshape(nseq, rows, d)
    hb = _modulated_norm(x1, mod[:, :, 3 * d:5 * d], g2_ref[...], d).reshape(m, d).astype(BF16)

    if carry:
        i = pl.program_id(1)

        @pl.when(i == 0)
        def _():
            gbuf_ref[:, 6:8, :] = jnp.zeros((nseq, 2, f), F32)

        @pl.when(i > 0)
        def _():
            gbuf_ref[:, 6:8, :] = gbuf_ref[:, rows + 6:rows + 8, :]
    else:
        gbuf_ref[:, 6:8, :] = prev_ref[...]

    acc = jnp.zeros((m, d), F32)
    for fc in range(f // fchunk):
        sl = slice(fc * fchunk, (fc + 1) * fchunk)
        g = jnp.dot(hb, wg_ref[:, sl], preferred_element_type=F32).reshape(nseq, rows, fchunk)
        u = jnp.dot(hb, wu_ref[:, sl], preferred_element_type=F32).reshape(nseq, rows, fchunk)
        gbuf_ref[:, 8:, sl] = g
        wfc = wfc_ref[:, sl]
        gc = wfc[0:1, :][None] * gbuf_ref[:, 6:6 + rows, sl]
        gc = gc + wfc[1:2, :][None] * gbuf_ref[:, 7:7 + rows, sl]
        gc = gc + wfc[2:3, :][None] * g
        z = gc + bfc_ref[:, sl][None]
        act = (z * jax.nn.sigmoid(z)) * u
        acc = acc + jnp.dot(act.reshape(m, fchunk).astype(BF16), wd_ref[sl, :], preferred_element_type=F32)
        fst_ref[:, :, sl] = g[:, rows - 2:rows, :]
    y_ref[...] = x1 + gt2 * acc.reshape(nseq, rows, d)


def _ffn(x, mod, attn, bconv, prev, wts, *, carry, rows, nseq, fchunk):
    nb, t, d = x.shape
    f = wts["wg"].shape[1]
    if carry:
        grid = (nb, t // rows)
        tile = lambda c: pl.BlockSpec((1, rows, c), lambda b, i: (b, i, 0))
        per_seq = lambda r, c: pl.BlockSpec((1, r, c), lambda b, i: (b, 0, 0))
    else:
        grid = (1, nb // nseq)
        tile = lambda c: pl.BlockSpec((nseq, rows, c), lambda b, i: (i, 0, 0))
        per_seq = lambda r, c: pl.BlockSpec((nseq, r, c), lambda b, i: (i, 0, 0))
    in_specs = [tile(d), per_seq(1, mod.shape[-1]), tile(attn.shape[-1]), tile(bconv.shape[-1]),
                _const_spec(wts["wot"].shape), _const_spec(wts["wob"].shape), _const_spec((1, d)),
                _const_spec(wts["wg"].shape), _const_spec(wts["wu"].shape), _const_spec((CONV_K, f)),
                _const_spec((1, f)), _const_spec(wts["wd"].shape), per_seq(CONV_K - 1, f)]
    return pl.pallas_call(
        functools.partial(_ffn_kernel, carry=carry, fchunk=fchunk),
        grid=grid, in_specs=in_specs,
        out_specs=[tile(d), per_seq(CONV_K - 1, f)],
        out_shape=[jax.ShapeDtypeStruct((nb, t, d), F32), jax.ShapeDtypeStruct((nb, CONV_K - 1, f), F32)],
        scratch_shapes=[pltpu.VMEM((nseq, rows + 8, f), F32)],
        compiler_params=_params(2),
        name="ffn_prompt" if carry else "ffn_sample",
    )(x, mod, attn, bconv, wts["wot"], wts["wob"], wts["g2"], wts["wg"], wts["wu"], wts["wfc"], wts["bfc"],
      wts["wd"], prev)


def _rope_angles(pos):
    inv = ROPE_THETA ** (-jnp.arange(0, ROPE_DIM, 2, dtype=F32) / ROPE_DIM)
    ang = pos.astype(F32)[:, None] * inv[None, :]
    return jnp.cos(ang), jnp.sin(ang)


def _rope_tables(pos):
    cos, sin = _rope_angles(pos)
    t = pos.shape[0]
    pad1 = jnp.ones((t, HEAD_DIM - ROPE_DIM), F32)
    pad0 = jnp.zeros((t, HEAD_DIM - ROPE_DIM), F32)
    zh = jnp.zeros((t, ROPE_HALF), F32)
    c64 = jnp.concatenate([cos, cos, pad1], axis=1)
    a64 = jnp.concatenate([-sin, zh, pad0], axis=1)
    b64 = jnp.concatenate([zh, sin, pad0], axis=1)
    rep = LANES // HEAD_DIM
    return tuple(jnp.tile(x, (1, rep)) for x in (c64, a64, b64))


def _layer_weights(l, d, g_mix_norm, w_in, g_q, g_k, w_short_conv, w_o, g_ffn_norm, w_gate, w_up, w_ffn_conv,
                   b_ffn_conv, w_down):
    aw = d // 2
    cw = d - aw
    nh = aw // HEAD_DIM
    w = w_in[l]
    nqi = N_IDX_HEADS * IDX_DIM
    o_qi = 3 * aw
    o_ki = o_qi + nqi
    o_wi = o_ki + IDX_DIM
    o_cv = o_wi + N_IDX_HEADS
    widx = jnp.concatenate([w[:, o_qi:o_ki], w[:, o_wi:o_cv],
                            jnp.zeros((d, IDX_PAD - nqi - N_IDX_HEADS), F32)], axis=1)
    wih, wil = _split_bf16(widx)
    wki = w[:, o_ki:o_wi]
    wkih, wkil = _split_bf16(wki)
    pad_ki = lambda a: jnp.pad(a, ((0, 0), (0, LANES - IDX_DIM)))
    head = jnp.arange(aw) // HEAD_DIM
    bd = jnp.where(head[:, None] == head[None, :], 1.0 / HEAD_DIM, 0.0).astype(BF16)
    w_fm = jnp.concatenate([w[:, aw:o_cv], jnp.zeros((d, 2 * SUBLANES - N_IDX_HEADS), F32)], axis=1)
    wfm, wfml = _split_bf16(w_fm)
    return dict(
        gmix=g_mix_norm[l].reshape(1, d),
        wq=w[:, 0:aw].astype(BF16), wih=wih, wil=wil,
        wcv=w[:, o_cv:o_cv + 3 * cw].astype(BF16), bd=bd,
        wfm=wfm.T, wfml=wfml[:, o_qi - aw:].T,
        wkvs=w[:, aw:3 * aw].astype(BF16), wkih=pad_ki(wkih), wkils=pad_ki(wkil),
        gq=jnp.tile(g_q[l], nh).reshape(1, aw), gk=jnp.tile(g_k[l], nh).reshape(1, aw),
        gkc=g_k[l].reshape(HEAD_DIM, 1),
        wsc=w_short_conv[l],
        wot=w_o[l][0:aw].astype(BF16), wob=w_o[l][aw:].astype(BF16),
        g2=g_ffn_norm[l].reshape(1, d),
        wg=w_gate[l].astype(BF16), wu=w_up[l].astype(BF16), wfc=w_ffn_conv[l],
        bfc=b_ffn_conv[l].reshape(1, -1), wd=w_down[l].astype(BF16))


def _pick(n, candidates):
    for c in candidates:
        if n % c == 0:
            return c
    raise ValueError(f"no tile size in {candidates} divides {n}")


def kernel(x_prompt, x_sample, cache_k, cache_v, cache_kidx, state_conv, state_ffn_conv, page_table, c_prompt,
           c_sample, w_ada, b_ada, g_mix_norm, w_in, g_q, g_k, w_short_conv, w_o, g_ffn_norm, w_gate, w_up,
           w_ffn_conv, b_ffn_conv, w_down):
    nbp, seq, d = x_prompt.shape
    nbs, t_new, _ = x_sample.shape
    depth = w_in.shape[0]
    aw = d // 2
    cw = d - aw
    nh = aw // HEAD_DIM
    f = w_gate.shape[-1]
    npages = page_table.shape[1]
    past = npages * PAGE_SIZE
    n_pool = cache_k.shape[1]
    assert t_new == SUBLANES and cache_k.shape[2] == PAGE_SIZE and aw % LANES == 0 and nh % 2 == 0

    blk = _pick(seq, (256, 128))
    rows_in = _pick(seq, (512, 256, 128))
    rows_ffn = _pick(seq, (256, 128))
    nseq = _pick(nbs, (32, 16, 8, 4, 2, 1))
    cpp = _pick(npages, (8, 4, 2))
    fchunk = f // 2 if (f // 2) % LANES == 0 else f
    top_p = min(TOPK_MAX, seq // TOPK_FRAC)
    top_s = min(TOPK_MAX, (past + t_new) // TOPK_FRAC)

    pos_p = jnp.arange(seq)
    tabs_p = _rope_tables(pos_p)
    tabs_pt = tuple(a.T for a in _rope_angles(pos_p))
    tabs_s = _rope_tables(past + jnp.arange(t_new))
    zeros_ffn = jnp.zeros((nbp, CONV_K - 1, f), F32)

    xp, xs = x_prompt, x_sample
    outs = {k: [] for k in ("kp", "vp", "kip", "cp", "fp", "ks", "vs", "kis", "cs", "fs")}
    for l in range(depth):
        wts = _layer_weights(l, d, g_mix_norm, w_in, g_q, g_k, w_short_conv, w_o, g_ffn_norm, w_gate, w_up,
                             w_ffn_conv, b_ffn_conv, w_down)
        mod = _ada(jnp.concatenate([c_prompt, c_sample], axis=0), w_ada[l], b_ada[l])
        mod_p = mod[:nbp].reshape(nbp, 1, 6 * d)
        mod_s = mod[nbp:].reshape(nbs, 1, 6 * d)

        p = _inproj_prompt(xp, mod_p, wts, tabs_p, tabs_pt, rows=rows_in, blk=blk)
        attn_p = _pattn(p, top=top_p, blk=blk)
        xp, fst_p = _ffn(xp, mod_p, attn_p, p["bconv"], zeros_ffn, wts, carry=True, rows=rows_ffn, nseq=1,
                         fchunk=fchunk)

        k_t = cache_k[l].transpose(0, 2, 3, 1).reshape(n_pool, aw, PAGE_SIZE)
        v_t = cache_v[l].transpose(0, 2, 3, 1).reshape(n_pool, aw, PAGE_SIZE)
        kidx_t = cache_kidx[l].transpose(0, 2, 1)
        s = _inproj_sample(xs, mod_s, state_conv[l], wts, tabs_s, nseq=nseq)
        keys = _sidx(s, kidx_t, page_table)
        bias = _ssel(keys, top=top_s, nseq=nseq)
        attn_s = _sattn(s, bias, k_t, v_t, page_table, cpp=cpp)
        xs, fst_s = _ffn(xs, mod_s, attn_s, s["bconv"], state_ffn_conv[l], wts, carry=False, rows=t_new, nseq=nseq,
                         fchunk=fchunk)

        outs["kp"].append(p["kt"].reshape(nbp, nh, HEAD_DIM, seq).transpose(0, 3, 1, 2))
        outs["vp"].append(p["vt"].reshape(nbp, nh, HEAD_DIM, seq).transpose(0, 3, 1, 2))
        outs["kip"].append(p["kit"].transpose(0, 2, 1))
        outs["cp"].append(p["cstate"])
        outs["fp"].append(fst_p)
        outs["ks"].append(s["k"].reshape(nbs, t_new, nh, HEAD_DIM))
        outs["vs"].append(s["v"].reshape(nbs, t_new, nh, HEAD_DIM))
        outs["kis"].append(s["kidx"])
        outs["cs"].append(s["cstate"])
        outs["fs"].append(fst_s)

    st = lambda k: jnp.stack(outs[k])
    return (xp, xs, st("kp"), st("vp"), st("kip"), st("cp"), st("fp"),
            st("ks"), st("vs"), st("kis"), st("cs"), st("fs"))
```

```python
import functools
import math

import jax
import jax.numpy as jnp
from jax import lax
from jax.experimental import pallas as pl
from jax.experimental.pallas import tpu as pltpu

F32 = jnp.float32
BF16 = jnp.bfloat16
I32 = jnp.int32

HEAD_DIM = 64
IDX_DIM = 64
N_IDX_HEADS = 4
ROPE_DIM = 16
ROPE_HALF = ROPE_DIM // 2
ROPE_THETA = 500000.0
CONV_K = 3
TOPK_MAX = 256
TOPK_FRAC = 4
PAGE_SIZE = 128
EPS = 1e-6

LANES = 128
SUBLANES = 8
IDX_PAD = 384
QI3 = 256
NEG_BIAS = -1e30
INT_MIN = -(2 ** 31)
VMEM_LIMIT = 56 * 1024 * 1024
LOG2E = math.log2(math.e)

_NT = (((1,), (1,)), ((), ()))


def _sortable(x):
    b = lax.bitcast_convert_type(x, I32)
    return b ^ (lax.shift_right_arithmetic(b, 31) & 0x7FFFFFFF)


_NEG_INF_KEY = (0xFF800000 ^ 0x7FFFFFFF) - 2 ** 32


def _lane_low(shape):
    return (lax.broadcasted_iota(I32, shape, len(shape) - 1) % LANES) < (LANES // 2)


def _split_bf16(x):
    hi = x.astype(BF16)
    lo = (x - hi.astype(F32)).astype(BF16)
    return hi, lo


MAX_PREDS = 2


def _one_if(cond):
    return jnp.where(cond, 1.0, 0.0)


def _bit_search(count, nbits, start, ones, pred_of, accept):
    def step(i, t):
        cand = t | jnp.left_shift(ones, nbits - 1 - i)
        (n,) = count([pred_of(cand)])
        return jnp.where(accept(n), cand, t)

    return lax.fori_loop(0, nbits, step, start)


def _select_params(count, top, nbits_idx, ones):
    kf = float(top)
    zero = ones * 0
    (c0,) = count([lambda kk, idx, rs: _one_if(kk >= 0)])
    t0 = jnp.where(c0 >= kf, zero, zero + INT_MIN)
    t = _bit_search(count, 31, t0, ones, lambda cand: (lambda kk, idx, rs: _one_if(kk >= cand[rs])),
                    lambda n: n >= kf)
    cgt, cge = count([lambda kk, idx, rs: _one_if(kk > t[rs]), lambda kk, idx, rs: _one_if(kk >= t[rs])])
    r = kf - cgt

    def jsearch():
        return _bit_search(
            count, nbits_idx, zero, ones,
            lambda cand: (lambda kk, idx, rs: jnp.where(kk == t[rs], _one_if(idx < cand[rs]), 0.0)),
            lambda n: n < r)

    need = jnp.max(cge) > kf
    j = lax.cond(need, jsearch, lambda: zero + (2 ** nbits_idx))
    return t, j


def _bias_tile(kk, idx, t, j):
    tie = jnp.where(kk == t, jnp.where(idx <= j, 0.0, NEG_BIAS), NEG_BIAS)
    bias = jnp.where(kk > t, 0.0, tie)
    return jnp.where(kk == _NEG_INF_KEY, NEG_BIAS, bias)


def _const_spec(shape):
    nd = len(shape)
    return pl.BlockSpec(shape, lambda *_: (0,) * nd, pipeline_mode=pl.Buffered(1))


def _params(n_grid):
    return pltpu.CompilerParams(dimension_semantics=("arbitrary",) * n_grid, vmem_limit_bytes=VMEM_LIMIT)


def _ada_kernel(c_ref, w_ref, b_ref, o_ref):
    c = c_ref[...]
    a = c * jax.nn.sigmoid(c)
    o_ref[...] = jnp.dot(a, w_ref[...], preferred_element_type=F32,
                         precision=lax.Precision.HIGHEST) + b_ref[...]


def _ada(c_all, w_ada, b_ada):
    n, d = c_all.shape
    n6 = w_ada.shape[1]
    return pl.pallas_call(
        _ada_kernel,
        grid=(n6 // d,),
        in_specs=[pl.BlockSpec((n, d), lambda j: (0, 0)),
                  pl.BlockSpec((d, d), lambda j: (0, j)),
                  pl.BlockSpec((1, d), lambda j: (0, j))],
        out_specs=pl.BlockSpec((n, d), lambda j: (0, j)),
        out_shape=jax.ShapeDtypeStruct((n, n6), F32),
        compiler_params=_params(1),
        name="ada",
    )(c_all, w_ada, b_ada.reshape(1, n6))


def _modulated_norm(x, mod, g, d):
    ms = jnp.mean(x * x, axis=-1, keepdims=True)
    h = (x * lax.rsqrt(ms + EPS)) * g[None]
    return h * (1.0 + mod[:, :, d:2 * d]) + mod[:, :, 0:d]


def _rope_lanes(x2, nseq, rows, cos, sa, sb):
    m, w = x2.shape
    xm = pltpu.roll(x2, w - ROPE_HALF, 1)
    xp = pltpu.roll(x2, ROPE_HALF, 1)
    y = (x2.reshape(nseq, rows, w) * cos[None] + xm.reshape(nseq, rows, w) * sa[None]
         + xp.reshape(nseq, rows, w) * sb[None])
    return y.reshape(m, w)


def _rope_rows(x3, cos, sin):
    x1 = x3[:, 0:ROPE_HALF, :]
    x2 = x3[:, ROPE_HALF:ROPE_DIM, :]
    return jnp.concatenate([x1 * cos[None] - x2 * sin[None], x2 * cos[None] + x1 * sin[None],
                            x3[:, ROPE_DIM:, :]], axis=1)


def _qexp_output(q, nseq, rows, tabs, bd_ref, gq, qexp_ref):
    m, aw = q.shape
    nh = aw // HEAD_DIM
    cos, sa, sb = tabs
    reps = aw // LANES
    ss = jnp.dot((q * q).astype(BF16), bd_ref[...], preferred_element_type=F32)
    q = (q * lax.rsqrt(ss + EPS)) * gq
    q = _rope_lanes(q, nseq, rows, jnp.concatenate([cos] * reps, axis=1), jnp.concatenate([sa] * reps, axis=1),
                    jnp.concatenate([sb] * reps, axis=1))
    low = _lane_low((1, LANES))
    qs = q * (HEAD_DIM ** -0.5 * LOG2E)
    pieces = []
    for hh in range(nh):
        chunk = qs[:, (hh // 2) * LANES:(hh // 2 + 1) * LANES]
        keep = low if hh % 2 == 0 else jnp.logical_not(low)
        pieces.append(jnp.where(keep, chunk, 0.0))
    qexp_ref[...] = jnp.concatenate(pieces, axis=1).reshape(nseq, rows, nh * LANES).astype(qexp_ref.dtype)


def _q_outputs(q, pidx, nseq, rows, tabs, bd_ref, gq, qexp_ref, qi3_ref, wi_ref):
    m = q.shape[0]
    cos, sa, sb = tabs
    low = _lane_low((1, LANES))
    _qexp_output(q, nseq, rows, tabs, bd_ref, gq, qexp_ref)

    nqi = N_IDX_HEADS * IDX_DIM
    qi = _rope_lanes(pidx[:, 0:nqi], nseq, rows, jnp.concatenate([cos] * (nqi // LANES), axis=1),
                     jnp.concatenate([sa] * (nqi // LANES), axis=1), jnp.concatenate([sb] * (nqi // LANES), axis=1))
    wi_ref[...] = (pidx[:, nqi:nqi + LANES] * ((IDX_DIM ** -0.5) * (N_IDX_HEADS ** -0.5))).reshape(nseq, rows, LANES)
    zero = jnp.zeros((m, LANES), F32)
    q3 = []
    for t in range(nqi // LANES):
        xt = qi[:, t * LANES:(t + 1) * LANES]
        hi = xt.astype(BF16).astype(F32)
        lo = xt - hi
        hi_r = pltpu.roll(hi, LANES // 2, 1)
        lo_r = pltpu.roll(lo, LANES // 2, 1)
        q3 += [jnp.where(low, hi, hi_r), jnp.where(low, lo, zero),
               jnp.where(low, hi_r, hi), jnp.where(low, lo_r, zero)]
    qi3_ref[...] = jnp.concatenate(q3, axis=1).reshape(nseq, rows, N_IDX_HEADS * QI3).astype(qi3_ref.dtype)


def _conv_branch(pcv, nseq, rows, cw, wsc_ref, cbuf_ref, bconv_ref, cst_ref):
    bg = pcv[:, 0:cw].reshape(nseq, rows, cw)
    cu = (pcv[:, cw:2 * cw] * pcv[:, 2 * cw:3 * cw]).reshape(nseq, rows, cw)
    cbuf_ref[:, 8:, :] = cu
    wsc = wsc_ref[...]
    conv = wsc[0:1, :][None] * cbuf_ref[:, 6:6 + rows, :]
    conv = conv + wsc[1:2, :][None] * cbuf_ref[:, 7:7 + rows, :]
    conv = conv + wsc[2:3, :][None] * cu
    bconv_ref[...] = (bg * conv).astype(bconv_ref.dtype)
    cst_ref[...] = cu[:, rows - 2:rows, :]


def _inproj_p_kernel(x_ref, mod_ref, gmix_ref, wq_ref, wfm_ref, wfml_ref, wkih_ref, wkil_ref, wcv_ref, bd_ref,
                     gq_ref, gkc_ref, cos_ref, sa_ref, sb_ref, cost_ref, sint_ref, wsc_ref,
                     qexp_ref, qi3t_ref, wit_ref, ki3_ref, bconv_ref, cst_ref, kt_ref, vt_ref, kit_ref, ktb_ref,
                     vta_ref, cbuf_ref):
    _, rows, d = x_ref.shape
    aw = kt_ref.shape[1]
    cw = bconv_ref.shape[-1]
    nh = aw // HEAD_DIM
    nblk, _, blk = ktb_ref.shape[1:]
    nqi = N_IDX_HEADS * IDX_DIM

    h2 = _modulated_norm(x_ref[...], mod_ref[...], gmix_ref[...], d).reshape(rows, d)
    hb, hl = _split_bf16(h2)
    cos, sa, sb = cos_ref[...], sa_ref[...], sb_ref[...]
    cost = cost_ref[...]
    sint = sint_ref[...]

    _qexp_output(jnp.dot(hb, wq_ref[...], preferred_element_type=F32), 1, rows, (cos, sa, sb), bd_ref, gq_ref[...],
                 qexp_ref)

    fm = lax.dot_general(wfm_ref[...], hb, _NT, preferred_element_type=F32)
    o_i = 2 * aw
    idx = (fm[o_i:] + lax.dot_general(wfm_ref[o_i:, :], hl, _NT, preferred_element_type=F32)
           + lax.dot_general(wfml_ref[...], hb, _NT, preferred_element_type=F32))

    k3 = fm[0:aw].reshape(nh, HEAD_DIM, rows)
    ss = jnp.mean(k3 * k3, axis=1, keepdims=True)
    k3 = _rope_rows((k3 * lax.rsqrt(ss + EPS)) * gkc_ref[...][None], cost, sint)
    kt = k3.reshape(aw, rows)
    vt = fm[aw:2 * aw]
    qi = _rope_rows(idx[0:nqi].reshape(N_IDX_HEADS, IDX_DIM, rows), cost, sint)
    kit = _rope_rows(idx[nqi:nqi + IDX_DIM].reshape(1, IDX_DIM, rows), cost, sint).reshape(IDX_DIM, rows)
    kt_ref[0] = kt
    vt_ref[0] = vt
    kit_ref[0] = kit
    wit_ref[0] = idx[nqi + IDX_DIM:nqi + IDX_DIM + SUBLANES] * ((IDX_DIM ** -0.5) * (N_IDX_HEADS ** -0.5))

    zeros = jnp.zeros((HEAD_DIM, rows), BF16)
    ones = jnp.ones((HEAD_DIM, rows), BF16)
    qih = qi.astype(BF16)
    qil = (qi - qih.astype(F32)).astype(BF16)
    qi3t_ref[0] = jnp.concatenate(
        [b_ for hh in range(N_IDX_HEADS) for b_ in (qih[hh], qih[hh], qil[hh], zeros)], axis=0)
    ktb = kt.astype(BF16)
    vtb = vt.astype(BF16)
    vta = jnp.concatenate(
        [b_ for hh in range(nh) for b_ in
         ((vtb[hh * HEAD_DIM:(hh + 1) * HEAD_DIM], ones) if hh % 2 == 0 else
          (ones, vtb[hh * HEAD_DIM:(hh + 1) * HEAD_DIM]))], axis=0)
    for bb in range(nblk):
        cs = slice(bb * blk, (bb + 1) * blk)
        ktb_ref[0, bb] = ktb[:, cs]
        vta_ref[0, bb] = vta[:, cs]

    ki = (jnp.dot(hb, wkih_ref[...], preferred_element_type=F32)
          + jnp.dot(hl, wkih_ref[...], preferred_element_type=F32)
          + jnp.dot(hb, wkil_ref[...], preferred_element_type=F32))
    low = _lane_low((1, LANES))
    ki = _rope_lanes(ki, 1, rows, jnp.where(low, cos, 1.0), jnp.where(low, sa, 0.0), jnp.where(low, sb, 0.0))
    khi = ki.astype(BF16).astype(F32)
    klo = ki - khi
    zero = jnp.zeros((rows, LANES), F32)
    ki3_ref[0] = jnp.concatenate([jnp.where(low, khi, pltpu.roll(klo, LANES // 2, 1)), jnp.where(low, khi, zero)],
                                 axis=1).astype(BF16)

    i = pl.program_id(1)

    @pl.when(i == 0)
    def _():
        cbuf_ref[:, 6:8, :] = jnp.zeros((1, 2, cw), F32)

    @pl.when(i > 0)
    def _():
        cbuf_ref[:, 6:8, :] = cbuf_ref[:, rows + 6:rows + 8, :]

    _conv_branch(jnp.dot(hb, wcv_ref[...], preferred_element_type=F32), 1, rows, cw, wsc_ref, cbuf_ref,
                 bconv_ref, cst_ref)


def _inproj_prompt(x, mod, wts, tabs, tabs_t, *, rows, blk):
    nb, t, d = x.shape
    aw = d // 2
    cw = d - aw
    nh = aw // HEAD_DIM
    nblk = rows // blk
    tile = lambda c: pl.BlockSpec((1, rows, c), lambda b, i: (b, i, 0))
    per_seq = lambda r, c: pl.BlockSpec((1, r, c), lambda b, i: (b, 0, 0))
    ftile = lambda r: pl.BlockSpec((1, r, rows), lambda b, i: (b, 0, i))
    btile = lambda r: pl.BlockSpec((1, nblk, r, blk), lambda b, i: (b, i, 0, 0))
    tab = pl.BlockSpec((rows, LANES), lambda b, i: (i, 0))
    tab_t = pl.BlockSpec((ROPE_HALF, rows), lambda b, i: (0, i))
    cos, sa, sb = tabs
    cost, sint = tabs_t
    in_specs = [tile(d), per_seq(1, mod.shape[-1]), _const_spec((1, d)),
                _const_spec(wts["wq"].shape), _const_spec(wts["wfm"].shape), _const_spec(wts["wfml"].shape),
                _const_spec(wts["wkih"].shape), _const_spec(wts["wkils"].shape), _const_spec(wts["wcv"].shape),
                _const_spec(wts["bd"].shape), _const_spec((1, aw)), _const_spec((HEAD_DIM, 1)),
                tab, tab, tab, tab_t, tab_t, _const_spec((CONV_K, cw))]
    out_specs = [tile(nh * LANES), ftile(N_IDX_HEADS * QI3), ftile(SUBLANES), tile(QI3), tile(cw),
                 per_seq(CONV_K - 1, cw), ftile(aw), ftile(aw), ftile(IDX_DIM), btile(aw), btile(nh * LANES)]
    sds = jax.ShapeDtypeStruct
    out_shape = [sds((nb, t, nh * LANES), BF16), sds((nb, N_IDX_HEADS * QI3, t), BF16), sds((nb, SUBLANES, t), F32),
                 sds((nb, t, QI3), BF16), sds((nb, t, cw), BF16), sds((nb, CONV_K - 1, cw), F32),
                 sds((nb, aw, t), F32), sds((nb, aw, t), F32), sds((nb, IDX_DIM, t), F32),
                 sds((nb, t // blk, aw, blk), BF16), sds((nb, t // blk, nh * LANES, blk), BF16)]
    names = ["qexp", "qi3t", "wit", "ki3", "bconv", "cstate", "kt", "vt", "kit", "ktb", "vta"]
    res = pl.pallas_call(
        _inproj_p_kernel,
        grid=(nb, t // rows), in_specs=in_specs, out_specs=out_specs, out_shape=out_shape,
        scratch_shapes=[pltpu.VMEM((1, rows + 8, cw), F32)],
        compiler_params=_params(2),
        name="inproj_prompt",
    )(x, mod, wts["gmix"], wts["wq"], wts["wfm"], wts["wfml"], wts["wkih"], wts["wkils"], wts["wcv"], wts["bd"],
      wts["gq"], wts["gkc"], cos, sa, sb, cost, sint, wts["wsc"])
    return dict(zip(names, res))


def _inproj_s_kernel(x_ref, mod_ref, gmix_ref, wq_ref, wih_ref, wil_ref, wcv_ref, wkvs_ref, wkih_ref, wkil_ref,
                     bd_ref, gq_ref, gk_ref, cos_ref, sa_ref, sb_ref, wsc_ref, prev_ref,
                     qexp_ref, qi3_ref, wi_ref, bconv_ref, cst_ref, k_ref, v_ref, kidx_ref, ki3_ref, cbuf_ref):
    nseq, rows, d = x_ref.shape
    m = nseq * rows
    aw = k_ref.shape[-1]
    cw = bconv_ref.shape[-1]
    h2 = _modulated_norm(x_ref[...], mod_ref[...], gmix_ref[...], d).reshape(m, d)
    hb, hl = _split_bf16(h2)
    cos, sa, sb = cos_ref[...], sa_ref[...], sb_ref[...]

    q = jnp.dot(hb, wq_ref[...], preferred_element_type=F32)
    pidx = (jnp.dot(hb, wih_ref[...], preferred_element_type=F32)
            + jnp.dot(hl, wih_ref[...], preferred_element_type=F32)
            + jnp.dot(hb, wil_ref[...], preferred_element_type=F32))
    _q_outputs(q, pidx, nseq, rows, (cos, sa, sb), bd_ref, gq_ref[...], qexp_ref, qi3_ref, wi_ref)

    kv = jnp.dot(hb, wkvs_ref[...], preferred_element_type=F32)
    k = kv[:, 0:aw]
    ss = jnp.dot((k * k).astype(BF16), bd_ref[...], preferred_element_type=F32)
    k = (k * lax.rsqrt(ss + EPS)) * gk_ref[...]
    reps = aw // LANES
    k = _rope_lanes(k, nseq, rows, jnp.concatenate([cos] * reps, axis=1), jnp.concatenate([sa] * reps, axis=1),
                    jnp.concatenate([sb] * reps, axis=1))
    k_ref[...] = k.reshape(nseq, rows, aw)
    v_ref[...] = kv[:, aw:2 * aw].reshape(nseq, rows, aw)

    ki = (jnp.dot(hb, wkih_ref[...], preferred_element_type=F32)
          + jnp.dot(hl, wkih_ref[...], preferred_element_type=F32)
          + jnp.dot(hb, wkil_ref[...], preferred_element_type=F32))
    low = _lane_low((1, LANES))
    ki = _rope_lanes(ki, nseq, rows, jnp.where(low, cos, 1.0), jnp.where(low, sa, 0.0), jnp.where(low, sb, 0.0))
    kidx_ref[...] = ki[:, 0:IDX_DIM].reshape(nseq, rows, IDX_DIM)
    khi = ki.astype(BF16).astype(F32)
    klo = ki - khi
    zero = jnp.zeros((m, LANES), F32)
    k3 = jnp.concatenate([jnp.where(low, khi, pltpu.roll(klo, LANES // 2, 1)), jnp.where(low, khi, zero)], axis=1)
    ki3_ref[...] = k3.reshape(nseq, rows, QI3)

    cbuf_ref[:, 6:8, :] = prev_ref[...]
    _conv_branch(jnp.dot(hb, wcv_ref[...], preferred_element_type=F32), nseq, rows, cw, wsc_ref, cbuf_ref,
                 bconv_ref, cst_ref)


def _inproj_sample(x, mod, prev, wts, tabs, *, nseq):
    nb, rows, d = x.shape
    aw = d // 2
    cw = d - aw
    nh = aw // HEAD_DIM
    tile = lambda c: pl.BlockSpec((nseq, rows, c), lambda i: (i, 0, 0))
    per_seq = lambda r, c: pl.BlockSpec((nseq, r, c), lambda i: (i, 0, 0))
    tab = pl.BlockSpec((rows, LANES), lambda i: (0, 0))
    cos, sa, sb = tabs
    in_specs = [tile(d), per_seq(1, mod.shape[-1]), _const_spec((1, d)),
                _const_spec(wts["wq"].shape), _const_spec(wts["wih"].shape), _const_spec(wts["wil"].shape),
                _const_spec(wts["wcv"].shape), _const_spec(wts["wkvs"].shape), _const_spec(wts["wkih"].shape),
                _const_spec(wts["wkils"].shape), _const_spec(wts["bd"].shape), _const_spec((1, aw)),
                _const_spec((1, aw)), tab, tab, tab, _const_spec((CONV_K, cw)), per_seq(CONV_K - 1, cw)]
    cols = [("qexp", nh * LANES), ("qi3", N_IDX_HEADS * QI3), ("wi", LANES), ("bconv", cw)]
    cols2 = [("k", aw), ("v", aw), ("kidx", IDX_DIM), ("ki3", QI3)]
    out_specs = ([tile(c) for _, c in cols] + [per_seq(CONV_K - 1, cw)] + [tile(c) for _, c in cols2])
    sds = jax.ShapeDtypeStruct
    out_shape = ([sds((nb, rows, c), F32) for _, c in cols] + [sds((nb, CONV_K - 1, cw), F32)]
                 + [sds((nb, rows, c), F32) for _, c in cols2])
    names = [n for n, _ in cols] + ["cstate"] + [n for n, _ in cols2]
    res = pl.pallas_call(
        _inproj_s_kernel,
        grid=(nb // nseq,), in_specs=in_specs, out_specs=out_specs, out_shape=out_shape,
        scratch_shapes=[pltpu.VMEM((nseq, rows + 8, cw), F32)],
        compiler_params=_params(1),
        name="inproj_sample",
    )(x, mod, wts["gmix"], wts["wq"], wts["wih"], wts["wil"], wts["wcv"], wts["wkvs"], wts["wkih"], wts["wkils"],
      wts["bd"], wts["gq"], wts["gk"], cos, sa, sb, wts["wsc"], prev)
    return dict(zip(names, res))


def _pattn_kernel(qexp_ref, qi3t_ref, wit_ref, ki3_ref, ktb_ref, vta_ref, o_ref, keys_ref, m_ref, acc_ref,
                  *, top, blk, seq):
    nh = acc_ref.shape[0]
    nlt = blk // LANES
    nsl = blk // SUBLANES
    qb = pl.program_id(1)
    nkb = qb + 1
    q0 = qb * blk
    wit = wit_ref[0]
    rowk = lax.broadcasted_iota(I32, (blk, blk), 0)
    colq = lax.broadcasted_iota(I32, (blk, blk), 1)
    sub = lax.broadcasted_iota(I32, (SUBLANES, blk), 0)
    ones = jnp.ones((SUBLANES, blk), I32)
    every = slice(None)

    def score_body(kb, carry_):
        off = pl.multiple_of(kb * blk, blk)
        kblk = ki3_ref[0, pl.ds(off, blk), :]
        acc = jnp.zeros((blk, blk), F32)
        for hh in range(N_IDX_HEADS):
            dd = jnp.dot(kblk, qi3t_ref[0, hh * QI3:(hh + 1) * QI3, :], preferred_element_type=F32)
            acc = acc + jnp.maximum(dd, 0.0) * wit[hh:hh + 1, :]
        acc = jnp.where(acc == 0.0, 0.0, acc)
        acc = jnp.where(kb * blk + rowk <= q0 + colq, acc, -jnp.inf)
        keys_ref[kb] = _sortable(acc)
        return carry_

    lax.fori_loop(0, nkb, score_body, 0)

    nchain = 4

    def count(preds):
        def body(kb, cs):
            cs = [list(c) for c in cs]
            for g in range(nsl):
                kk = keys_ref[kb, g * SUBLANES:(g + 1) * SUBLANES, :]
                idx = kb * blk + g * SUBLANES + sub
                for a, pred in enumerate(preds):
                    cs[a][g % nchain] = cs[a][g % nchain] + pred(kk, idx, every)
            return tuple(tuple(c) for c in cs)

        zero_c = tuple(jnp.zeros((SUBLANES, blk), F32) for _ in range(nchain))
        cs = lax.fori_loop(0, nkb, body, tuple(zero_c for _ in preds))
        return [jnp.sum(sum(c[1:], c[0]), axis=0, keepdims=True) for c in cs]

    t, j = _select_params(count, top, (seq - 1).bit_length(), ones)

    m_ref[...] = jnp.full(m_ref.shape, -jnp.inf, F32)
    acc_ref[...] = jnp.zeros(acc_ref.shape, F32)
    t_full = jnp.concatenate([t] * nsl, axis=0)
    j_full = jnp.concatenate([j] * nsl, axis=0)

    def attn_body(kb, carry_):
        bias = _bias_tile(keys_ref[kb], kb * blk + rowk, t_full, j_full).T
        for hh in range(nh):
            pp = hh // 2
            qh = qexp_ref[0, :, hh * LANES:(hh + 1) * LANES]
            kh = ktb_ref[0, kb, pp * LANES:(pp + 1) * LANES, :]
            va = vta_ref[0, kb, hh * LANES:(hh + 1) * LANES, :]
            s = jnp.dot(qh, kh, preferred_element_type=F32) + bias
            m_prev = m_ref[hh]
            m_new = jnp.maximum(m_prev, jnp.max(s, axis=1, keepdims=True))
            alpha = jnp.exp2(m_prev - m_new)
            p = jnp.exp2(s - jnp.concatenate([m_new] * nlt, axis=1))
            acc_ref[hh] = alpha * acc_ref[hh] + lax.dot_general(p.astype(BF16), va, _NT, preferred_element_type=F32)
            m_ref[hh] = m_new
        return carry_

    lax.fori_loop(0, nkb, attn_body, 0)

    low = _lane_low((1, LANES))
    outs = []
    for pp in range(nh // 2):
        a0 = acc_ref[2 * pp]
        a1 = acc_ref[2 * pp + 1]
        outs.append(jnp.where(low, a0 / pltpu.roll(a0, LANES // 2, 1), a1 / pltpu.roll(a1, LANES // 2, 1)))
    o_ref[0] = jnp.concatenate(outs, axis=1).astype(o_ref.dtype)


def _pattn(p, *, top, blk):
    nb, nblk, aw, _ = p["ktb"].shape
    seq = nblk * blk
    nh = aw // HEAD_DIM
    qtile = lambda c: pl.BlockSpec((1, blk, c), lambda b, i: (b, i, 0))
    ftile = lambda r: pl.BlockSpec((1, r, blk), lambda b, i: (b, 0, i))
    blocks = lambda r: pl.BlockSpec((1, nblk, r, blk), lambda b, i: (b, 0, 0, 0))
    return pl.pallas_call(
        functools.partial(_pattn_kernel, top=top, blk=blk, seq=seq),
        grid=(nb, nblk),
        in_specs=[qtile(nh * LANES), ftile(N_IDX_HEADS * QI3), ftile(SUBLANES),
                  pl.BlockSpec((1, seq, QI3), lambda b, i: (b, 0, 0)), blocks(aw), blocks(nh * LANES)],
        out_specs=qtile(aw),
        out_shape=jax.ShapeDtypeStruct((nb, seq, aw), BF16),
        scratch_shapes=[pltpu.VMEM((nblk, blk, blk), I32),
                        pltpu.VMEM((nh, blk, LANES), F32),
                        pltpu.VMEM((nh, blk, LANES), F32)],
        compiler_params=_params(2),
        name="pattn",
    )(p["qexp"], p["qi3t"], p["wit"], p["ki3"], p["ktb"], p["vta"])


def _sidx_kernel(pt_ref, qi3_ref, wi_ref, kin_ref, kidx_hbm, keys_ref, buf_ref, sem_ref, *, npages):
    t_new = qi3_ref.shape[1]
    b = pl.program_id(0)
    nb = pl.num_programs(0)
    slot = lax.rem(b, 2)

    def page_copy(seq_i, i, sl):
        return pltpu.make_async_copy(kidx_hbm.at[pt_ref[seq_i, i]], buf_ref.at[sl, i], sem_ref.at[sl])

    @pl.when(b == 0)
    def _():
        for i in range(npages):
            page_copy(0, i, 0).start()

    @pl.when(b + 1 < nb)
    def _():
        for i in range(npages):
            page_copy(b + 1, i, 1 - slot).start()

    qi3 = qi3_ref[0]
    wi = wi_ref[0]
    heads = range(N_IDX_HEADS)
    lhs = jnp.concatenate([qi3[:, hh * QI3:hh * QI3 + LANES] for hh in heads]
                          + [qi3[:, hh * QI3 + LANES:(hh + 1) * QI3] for hh in heads], axis=0).astype(BF16)

    def head_sum(dd):
        acc = jnp.zeros((t_new, LANES), F32)
        for hh in heads:
            acc = acc + jnp.maximum(dd[hh * t_new:(hh + 1) * t_new], 0.0) * wi[:, hh:hh + 1]
        return jnp.where(acc == 0.0, 0.0, acc)

    for i in range(npages):
        page_copy(b, i, slot).wait()

    group = math.gcd(npages, 8)

    def page_body(g, carry_):
        for u in range(group):
            i = g * group + u
            kp = buf_ref[slot, i]
            kh, kl = _split_bf16(kp)
            d2 = jnp.dot(lhs, jnp.concatenate([kh, kl], axis=0), preferred_element_type=F32)
            nq = N_IDX_HEADS * t_new
            keys_ref[0, i] = _sortable(head_sum(d2[0:nq] + d2[nq:2 * nq]))
        return carry_

    lax.fori_loop(0, npages // group, page_body, 0)

    q3 = jnp.concatenate([qi3[:, hh * QI3:(hh + 1) * QI3] for hh in heads], axis=0).astype(BF16)
    kn = jnp.concatenate([kin_ref[0], jnp.zeros((LANES - t_new, QI3), F32)], axis=0).astype(BF16)
    sc = head_sum(lax.dot_general(q3, kn, _NT, preferred_element_type=F32))
    rown = lax.broadcasted_iota(I32, (t_new, LANES), 0)
    lanen = lax.broadcasted_iota(I32, (t_new, LANES), 1)
    keys_ref[0, npages] = _sortable(jnp.where(lanen <= rown, sc, -jnp.inf))


def _sidx(s, kidx_t, page_table):
    nb, t_new, _ = s["qi3"].shape
    npages = page_table.shape[1]
    seq_blk = lambda c: pl.BlockSpec((1, t_new, c), lambda b, pt: (b, 0, 0))
    grid_spec = pltpu.PrefetchScalarGridSpec(
        num_scalar_prefetch=1, grid=(nb,),
        in_specs=[seq_blk(N_IDX_HEADS * QI3), seq_blk(LANES), seq_blk(QI3), pl.BlockSpec(memory_space=pl.ANY)],
        out_specs=pl.BlockSpec((1, npages + 1, t_new, LANES), lambda b, pt: (b, 0, 0, 0)),
        scratch_shapes=[pltpu.VMEM((2, npages, IDX_DIM, PAGE_SIZE), F32), pltpu.SemaphoreType.DMA((2,))])
    return pl.pallas_call(
        functools.partial(_sidx_kernel, npages=npages),
        grid_spec=grid_spec,
        out_shape=jax.ShapeDtypeStruct((nb, npages + 1, t_new, LANES), I32),
        compiler_params=_params(1),
        name="sidx",
    )(page_table, s["qi3"], s["wi"], s["ki3"], kidx_t)


def _ssel_kernel(keys_ref, bias_ref, cnt_ref, *, top):
    nseq, np1, t_new, _ = keys_ref.shape
    rows = nseq * t_new
    lane = lax.broadcasted_iota(I32, (rows, LANES), 1)
    ones = jnp.ones((rows, LANES), I32)
    cseq = math.gcd(nseq, 16)
    crow = cseq * t_new
    lane_c = lax.broadcasted_iota(I32, (crow, LANES), 1)

    def count(preds):
        na = len(preds)
        cnt_ref[0:na] = jnp.zeros((na,) + cnt_ref.shape[1:], F32)

        def body(pg, carry_):
            idx = pg * LANES + lane_c
            for ch in range(nseq // cseq):
                rs = slice(ch * crow, (ch + 1) * crow)
                kk = keys_ref[ch * cseq:(ch + 1) * cseq, pg].reshape(crow, LANES)
                for a in range(na):
                    cnt_ref[a, rs, :] = cnt_ref[a, rs, :] + preds[a](kk, idx, rs)
            return carry_

        lax.fori_loop(0, np1, body, 0)
        return [jnp.sum(cnt_ref[a], axis=1, keepdims=True) for a in range(na)]

    t, j = _select_params(count, top, (np1 * LANES - 1).bit_length(), ones)

    def bias_body(pg, carry_):
        bias_ref[:, pg] = _bias_tile(keys_ref[:, pg].reshape(rows, LANES), pg * LANES + lane, t, j).reshape(
            nseq, t_new, LANES)
        return carry_

    lax.fori_loop(0, np1, bias_body, 0)


def _ssel(keys, *, top, nseq):
    nb, np1, t_new, _ = keys.shape
    blk_spec = pl.BlockSpec((nseq, np1, t_new, LANES), lambda i: (i, 0, 0, 0))
    return pl.pallas_call(
        functools.partial(_ssel_kernel, top=top),
        grid=(nb // nseq,), in_specs=[blk_spec], out_specs=blk_spec,
        out_shape=jax.ShapeDtypeStruct(keys.shape, F32),
        scratch_shapes=[pltpu.VMEM((MAX_PREDS, nseq * t_new, LANES), F32)],
        compiler_params=_params(1),
        name="ssel",
    )(keys)


def _sattn_kernel(pt_ref, qexp_ref, bias_ref, kn_ref, vn_ref, k_hbm, v_hbm, o_ref,
                  kbuf_ref, vbuf_ref, sem_ref, qbd_ref, m_ref, l_ref, acc_ref, *, cpp, nchunks):
    spb, t_new, aw = kn_ref.shape
    nh = aw // HEAD_DIM
    npair = nh // 2
    b = pl.program_id(0)
    c = pl.program_id(1)
    step = b * nchunks + c
    total = pl.num_programs(0) * nchunks
    slot = lax.rem(step, 2)

    def copies(grp, ch, sl):
        out = []
        for q_ in range(spb):
            for i in range(cpp):
                pg = pt_ref[grp * spb + q_, ch * cpp + i]
                out.append(pltpu.make_async_copy(k_hbm.at[pg], kbuf_ref.at[sl, q_, i], sem_ref.at[0, sl]))
                out.append(pltpu.make_async_copy(v_hbm.at[pg], vbuf_ref.at[sl, q_, i], sem_ref.at[1, sl]))
        return out

    @pl.when(step == 0)
    def _():
        for cp in copies(0, 0, 0):
            cp.start()

    @pl.when(step + 1 < total)
    def _():
        nxt = step + 1
        for cp in copies(nxt // nchunks, lax.rem(nxt, nchunks), 1 - slot):
            cp.start()

    @pl.when(c == 0)
    def _():
        zero = jnp.zeros((t_new, LANES), F32)
        for q_ in range(spb):
            qe = qexp_ref[q_]
            rows_ = []
            for hh in range(nh):
                chunk = qe[:, hh * LANES:(hh + 1) * LANES]
                rows_.append(jnp.concatenate([chunk if tt == hh // 2 else zero for tt in range(npair)], axis=1))
            qbd_ref[q_] = jnp.concatenate(rows_, axis=0).astype(BF16)
        m_ref[...] = jnp.full(m_ref.shape, -jnp.inf, F32)
        l_ref[...] = jnp.zeros(l_ref.shape, F32)
        acc_ref[...] = jnp.zeros(acc_ref.shape, F32)

    def update(q_, s, pv_fn):
        m_prev = m_ref[q_]
        m_new = jnp.maximum(m_prev, jnp.max(s, axis=1, keepdims=True))
        alpha = jnp.exp2(m_prev - m_new)
        p = jnp.exp2(s - jnp.concatenate([m_new] * (s.shape[1] // LANES), axis=1))
        l_ref[q_] = alpha * l_ref[q_] + jnp.sum(p, axis=1, keepdims=True)
        acc_ref[q_] = jnp.concatenate([alpha] * (aw // LANES), axis=1) * acc_ref[q_] + pv_fn(p.astype(BF16))
        m_ref[q_] = m_new

    for cp in copies(b, c, slot):
        cp.wait()

    for q_ in range(spb):
        kall = jnp.concatenate([kbuf_ref[slot, q_, i] for i in range(cpp)], axis=1).astype(BF16)
        vall = jnp.concatenate([vbuf_ref[slot, q_, i] for i in range(cpp)], axis=1).astype(BF16)
        ball = jnp.concatenate([bias_ref[q_, c * cpp + i] for i in range(cpp)], axis=1)
        s_all = jnp.dot(qbd_ref[q_], kall, preferred_element_type=F32) + jnp.concatenate([ball] * nh, axis=0)
        update(q_, s_all, lambda pb, vall=vall: lax.dot_general(pb, vall, _NT, preferred_element_type=F32))

    @pl.when(c == nchunks - 1)
    def _():
        pad = jnp.zeros((LANES - t_new, aw), F32)
        low = _lane_low((1, LANES))
        for q_ in range(spb):
            knp = jnp.concatenate([kn_ref[q_], pad], axis=0).astype(BF16)
            vnp = jnp.concatenate([vn_ref[q_], pad], axis=0).astype(BF16)
            s = (lax.dot_general(qbd_ref[q_], knp, _NT, preferred_element_type=F32)
                 + jnp.concatenate([bias_ref[q_, nchunks * cpp]] * nh, axis=0))
            update(q_, s, lambda pb, vnp=vnp: jnp.dot(pb, vnp, preferred_element_type=F32))
            acc = acc_ref[q_]
            inv = 1.0 / l_ref[q_]
            outs = []
            for pp in range(npair):
                r0 = slice(2 * pp * t_new, (2 * pp + 1) * t_new)
                r1 = slice((2 * pp + 1) * t_new, (2 * pp + 2) * t_new)
                cs = slice(pp * LANES, (pp + 1) * LANES)
                outs.append(jnp.where(low, acc[r0, cs] * inv[r0], acc[r1, cs] * inv[r1]))
            o_ref[q_] = jnp.concatenate(outs, axis=1)


def _sattn(s, bias, k_t, v_t, page_table, *, cpp, spb):
    nb, t_new, aw = s["k"].shape
    nh = aw // HEAD_DIM
    npages = page_table.shape[1]
    nchunks = npages // cpp
    seq_blk = lambda c: pl.BlockSpec((spb, t_new, c), lambda b, ch, pt: (b, 0, 0))
    grid_spec = pltpu.PrefetchScalarGridSpec(
        num_scalar_prefetch=1, grid=(nb // spb, nchunks),
        in_specs=[seq_blk(nh * LANES),
                  pl.BlockSpec((spb, npages + 1, t_new, LANES), lambda b, ch, pt: (b, 0, 0, 0)),
                  seq_blk(aw), seq_blk(aw),
                  pl.BlockSpec(memory_space=pl.ANY), pl.BlockSpec(memory_space=pl.ANY)],
        out_specs=seq_blk(aw),
        scratch_shapes=[pltpu.VMEM((2, spb, cpp, aw, PAGE_SIZE), F32),
                        pltpu.VMEM((2, spb, cpp, aw, PAGE_SIZE), F32),
                        pltpu.SemaphoreType.DMA((2, 2)),
                        pltpu.VMEM((spb, nh * t_new, aw), BF16),
                        pltpu.VMEM((spb, nh * t_new, LANES), F32),
                        pltpu.VMEM((spb, nh * t_new, LANES), F32),
                        pltpu.VMEM((spb, nh * t_new, aw), F32)])
    return pl.pallas_call(
        functools.partial(_sattn_kernel, cpp=cpp, nchunks=nchunks),
        grid_spec=grid_spec,
        out_shape=jax.ShapeDtypeStruct((nb, t_new, aw), F32),
        compiler_params=_params(2),
        name="sattn",
    )(page_table, s["qexp"], bias, s["k"], s["v"], k_t, v_t)


def _ffn_kernel(x_ref, mod_ref, attn_ref, bconv_ref, wot_ref, wob_ref, g2_ref, wg_ref, wu_ref, wfc_ref, bfc_ref,
                wd_ref, prev_ref, y_ref, fst_ref, gbuf_ref, *, carry, fchunk):
    nseq, rows, d = x_ref.shape
    m = nseq * rows
    f = wg_ref.shape[1]
    x = x_ref[...]
    mod = mod_ref[...]
    gt1 = mod[:, :, 2 * d:3 * d]
    gt2 = mod[:, :, 5 * d:6 * d]
    a = attn_ref[...].reshape(m, attn_ref.shape[-1]).astype(BF16)
    bc = bconv_ref[...].reshape(m, bconv_ref.shape[-1]).astype(BF16)
    mix = (jnp.dot(a, wot_ref[...], preferred_element_type=F32)
           + jnp.dot(bc, wob_ref[...], preferred_element_type=F32))
    x1 = x + gt1 * mix.reshape(nseq, rows, d)
    hb = _modulated_norm(x1, mod[:, :, 3 * d:5 * d], g2_ref[...], d).reshape(m, d).astype(BF16)

    if carry:
        i = pl.program_id(1)

        @pl.when(i == 0)
        def _():
            gbuf_ref[:, 6:8, :] = jnp.zeros((nseq, 2, f), F32)

        @pl.when(i > 0)
        def _():
            gbuf_ref[:, 6:8, :] = gbuf_ref[:, rows + 6:rows + 8, :]
    else:
        gbuf_ref[:, 6:8, :] = prev_ref[...]

    acc = jnp.zeros((m, d), F32)
    for fc in range(f // fchunk):
        sl = slice(fc * fchunk, (fc + 1) * fchunk)
        g = jnp.dot(hb, wg_ref[:, sl], preferred_element_type=F32).reshape(nseq, rows, fchunk)
        u = jnp.dot(hb, wu_ref[:, sl], preferred_element_type=F32).reshape(nseq, rows, fchunk)
        gbuf_ref[:, 8:, sl] = g
        wfc = wfc_ref[:, sl]
        gc = wfc[0:1, :][None] * gbuf_ref[:, 6:6 + rows, sl]
        gc = gc + wfc[1:2, :][None] * gbuf_ref[:, 7:7 + rows, sl]
        gc = gc + wfc[2:3, :][None] * g
        z = gc + bfc_ref[:, sl][None]
        act = (z * jax.nn.sigmoid(z)) * u
        acc = acc + jnp.dot(act.reshape(m, fchunk).astype(BF16), wd_ref[sl, :], preferred_element_type=F32)
        fst_ref[:, :, sl] = g[:, rows - 2:rows, :]
    y_ref[...] = x1 + gt2 * acc.reshape(nseq, rows, d)


def _ffn(x, mod, attn, bconv, prev, wts, *, carry, rows, nseq, fchunk):
    nb, t, d = x.shape
    f = wts["wg"].shape[1]
    if carry:
        grid = (nb, t // rows)
        tile = lambda c: pl.BlockSpec((1, rows, c), lambda b, i: (b, i, 0))
        per_seq = lambda r, c: pl.BlockSpec((1, r, c), lambda b, i: (b, 0, 0))
    else:
        grid = (1, nb // nseq)
        tile = lambda c: pl.BlockSpec((nseq, rows, c), lambda b, i: (i, 0, 0))
        per_seq = lambda r, c: pl.BlockSpec((nseq, r, c), lambda b, i: (i, 0, 0))
    in_specs = [tile(d), per_seq(1, mod.shape[-1]), tile(attn.shape[-1]), tile(bconv.shape[-1]),
                _const_spec(wts["wot"].shape), _const_spec(wts["wob"].shape), _const_spec((1, d)),
                _const_spec(wts["wg"].shape), _const_spec(wts["wu"].shape), _const_spec((CONV_K, f)),
                _const_spec((1, f)), _const_spec(wts["wd"].shape), per_seq(CONV_K - 1, f)]
    return pl.pallas_call(
        functools.partial(_ffn_kernel, carry=carry, fchunk=fchunk),
        grid=grid, in_specs=in_specs,
        out_specs=[tile(d), per_seq(CONV_K - 1, f)],
        out_shape=[jax.ShapeDtypeStruct((nb, t, d), F32), jax.ShapeDtypeStruct((nb, CONV_K - 1, f), F32)],
        scratch_shapes=[pltpu.VMEM((nseq, rows + 8, f), F32)],
        compiler_params=_params(2),
        name="ffn_prompt" if carry else "ffn_sample",
    )(x, mod, attn, bconv, wts["wot"], wts["wob"], wts["g2"], wts["wg"], wts["wu"], wts["wfc"], wts["bfc"],
      wts["wd"], prev)


def _rope_angles(pos):
    inv = ROPE_THETA ** (-jnp.arange(0, ROPE_DIM, 2, dtype=F32) / ROPE_DIM)
    ang = pos.astype(F32)[:, None] * inv[None, :]
    return jnp.cos(ang), jnp.sin(ang)


def _rope_tables(pos):
    cos, sin = _rope_angles(pos)
    t = pos.shape[0]
    pad1 = jnp.ones((t, HEAD_DIM - ROPE_DIM), F32)
    pad0 = jnp.zeros((t, HEAD_DIM - ROPE_DIM), F32)
    zh = jnp.zeros((t, ROPE_HALF), F32)
    c64 = jnp.concatenate([cos, cos, pad1], axis=1)
    a64 = jnp.concatenate([-sin, zh, pad0], axis=1)
    b64 = jnp.concatenate([zh, sin, pad0], axis=1)
    rep = LANES // HEAD_DIM
    return tuple(jnp.tile(x, (1, rep)) for x in (c64, a64, b64))


def _layer_weights(l, d, g_mix_norm, w_in, g_q, g_k, w_short_conv, w_o, g_ffn_norm, w_gate, w_up, w_ffn_conv,
                   b_ffn_conv, w_down):
    aw = d // 2
    cw = d - aw
    nh = aw // HEAD_DIM
    w = w_in[l]
    nqi = N_IDX_HEADS * IDX_DIM
    o_qi = 3 * aw
    o_ki = o_qi + nqi
    o_wi = o_ki + IDX_DIM
    o_cv = o_wi + N_IDX_HEADS
    widx = jnp.concatenate([w[:, o_qi:o_ki], w[:, o_wi:o_cv],
                            jnp.zeros((d, IDX_PAD - nqi - N_IDX_HEADS), F32)], axis=1)
    wih, wil = _split_bf16(widx)
    wki = w[:, o_ki:o_wi]
    wkih, wkil = _split_bf16(wki)
    pad_ki = lambda a: jnp.pad(a, ((0, 0), (0, LANES - IDX_DIM)))
    head = jnp.arange(aw) // HEAD_DIM
    bd = jnp.where(head[:, None] == head[None, :], 1.0 / HEAD_DIM, 0.0).astype(BF16)
    w_fm = jnp.concatenate([w[:, aw:o_cv], jnp.zeros((d, 2 * SUBLANES - N_IDX_HEADS), F32)], axis=1)
    wfm, wfml = _split_bf16(w_fm)
    return dict(
        gmix=g_mix_norm[l].reshape(1, d),
        wq=w[:, 0:aw].astype(BF16), wih=wih, wil=wil,
        wcv=w[:, o_cv:o_cv + 3 * cw].astype(BF16), bd=bd,
        wfm=wfm.T, wfml=wfml[:, o_qi - aw:].T,
        wkvs=w[:, aw:3 * aw].astype(BF16), wkih=pad_ki(wkih), wkils=pad_ki(wkil),
        gq=jnp.tile(g_q[l], nh).reshape(1, aw), gk=jnp.tile(g_k[l], nh).reshape(1, aw),
        gkc=g_k[l].reshape(HEAD_DIM, 1),
        wsc=w_short_conv[l],
        wot=w_o[l][0:aw].astype(BF16), wob=w_o[l][aw:].astype(BF16),
        g2=g_ffn_norm[l].reshape(1, d),
        wg=w_gate[l].astype(BF16), wu=w_up[l].astype(BF16), wfc=w_ffn_conv[l],
        bfc=b_ffn_conv[l].reshape(1, -1), wd=w_down[l].astype(BF16))


def _pick(n, candidates):
    for c in candidates:
        if n % c == 0:
            return c
    raise ValueError(f"no tile size in {candidates} divides {n}")


def kernel(x_prompt, x_sample, cache_k, cache_v, cache_kidx, state_conv, state_ffn_conv, page_table, c_prompt,
           c_sample, w_ada, b_ada, g_mix_norm, w_in, g_q, g_k, w_short_conv, w_o, g_ffn_norm, w_gate, w_up,
           w_ffn_conv, b_ffn_conv, w_down):
    nbp, seq, d = x_prompt.shape
    nbs, t_new, _ = x_sample.shape
    depth = w_in.shape[0]
    aw = d // 2
    cw = d - aw
    nh = aw // HEAD_DIM
    f = w_gate.shape[-1]
    npages = page_table.shape[1]
    past = npages * PAGE_SIZE
    n_pool = cache_k.shape[1]
    assert t_new == SUBLANES and cache_k.shape[2] == PAGE_SIZE and aw % LANES == 0 and nh % 2 == 0

    blk = _pick(seq, (256, 128))
    rows_in = _pick(seq, (512, 256, 128))
    rows_ffn = _pick(seq, (256, 128))
    nseq = _pick(nbs, (32, 16, 8, 4, 2, 1))
    cpp = _pick(npages, (8, 4, 2))
    fchunk = f // 2 if (f // 2) % LANES == 0 else f
    top_p = min(TOPK_MAX, seq // TOPK_FRAC)
    top_s = min(TOPK_MAX, (past + t_new) // TOPK_FRAC)

    pos_p = jnp.arange(seq)
    tabs_p = _rope_tables(pos_p)
    tabs_pt = tuple(a.T for a in _rope_angles(pos_p))
    tabs_s = _rope_tables(past + jnp.arange(t_new))
    zeros_ffn = jnp.zeros((nbp, CONV_K - 1, f), F32)

    xp, xs = x_prompt, x_sample
    outs = {k: [] for k in ("kp", "vp", "kip", "cp", "fp", "ks", "vs", "kis", "cs", "fs")}
    for l in range(depth):
        wts = _layer_weights(l, d, g_mix_norm, w_in, g_q, g_k, w_short_conv, w_o, g_ffn_norm, w_gate, w_up,
                             w_ffn_conv, b_ffn_conv, w_down)
        mod = _ada(jnp.concatenate([c_prompt, c_sample], axis=0), w_ada[l], b_ada[l])
        mod_p = mod[:nbp].reshape(nbp, 1, 6 * d)
        mod_s = mod[nbp:].reshape(nbs, 1, 6 * d)

        p = _inproj_prompt(xp, mod_p, wts, tabs_p, tabs_pt, rows=rows_in, blk=blk)
        attn_p = _pattn(p, top=top_p, blk=blk)
        xp, fst_p = _ffn(xp, mod_p, attn_p, p["bconv"], zeros_ffn, wts, carry=True, rows=rows_ffn, nseq=1,
                         fchunk=fchunk)

        k_t = cache_k[l].transpose(0, 2, 3, 1).reshape(n_pool, aw, PAGE_SIZE)
        v_t = cache_v[l].transpose(0, 2, 3, 1).reshape(n_pool, aw, PAGE_SIZE)
        kidx_t = cache_kidx[l].transpose(0, 2, 1)
        s = _inproj_sample(xs, mod_s, state_conv[l], wts, tabs_s, nseq=nseq)
        keys = _sidx(s, kidx_t, page_table)
        bias = _ssel(keys, top=top_s, nseq=nseq)
        attn_s = _sattn(s, bias, k_t, v_t, page_table, cpp=cpp, spb=_pick(nbs, (2, 1)))
        xs, fst_s = _ffn(xs, mod_s, attn_s, s["bconv"], state_ffn_conv[l], wts, carry=False, rows=t_new, nseq=nseq,
                         fchunk=fchunk)

        outs["kp"].append(p["kt"].reshape(nbp, nh, HEAD_DIM, seq).transpose(0, 3, 1, 2))
        outs["vp"].append(p["vt"].reshape(nbp, nh, HEAD_DIM, seq).transpose(0, 3, 1, 2))
        outs["kip"].append(p["kit"].transpose(0, 2, 1))
        outs["cp"].append(p["cstate"])
        outs["fp"].append(fst_p)
        outs["ks"].append(s["k"].reshape(nbs, t_new, nh, HEAD_DIM))
        outs["vs"].append(s["v"].reshape(nbs, t_new, nh, HEAD_DIM))
        outs["kis"].append(s["kidx"])
        outs["cs"].append(s["cstate"])
        outs["fs"].append(fst_s)

    st = lambda k: jnp.stack(outs[k])
    return (xp, xs, st("kp"), st("vp"), st("kip"), st("cp"), st("fp"),
            st("ks"), st("vs"), st("kis"), st("cs"), st("fs"))
```

```python
import functools
import math

import jax
import jax.numpy as jnp
from jax import lax
from jax.experimental import pallas as pl
from jax.experimental.pallas import tpu as pltpu

F32 = jnp.float32
BF16 = jnp.bfloat16
I32 = jnp.int32

HEAD_DIM = 64
IDX_DIM = 64
N_IDX_HEADS = 4
ROPE_DIM = 16
ROPE_HALF = ROPE_DIM // 2
ROPE_THETA = 500000.0
CONV_K = 3
TOPK_MAX = 256
TOPK_FRAC = 4
PAGE_SIZE = 128
EPS = 1e-6

LANES = 128
SUBLANES = 8
IDX_PAD = 384
QI3 = 256
NEG_BIAS = -1e30
INT_MIN = -(2 ** 31)
VMEM_LIMIT = 56 * 1024 * 1024
LOG2E = math.log2(math.e)

_NT = (((1,), (1,)), ((), ()))


def _sortable(x):
    b = lax.bitcast_convert_type(x, I32)
    return b ^ (lax.shift_right_arithmetic(b, 31) & 0x7FFFFFFF)


_NEG_INF_KEY = (0xFF800000 ^ 0x7FFFFFFF) - 2 ** 32


def _lane_low(shape):
    return (lax.broadcasted_iota(I32, shape, len(shape) - 1) % LANES) < (LANES // 2)


def _split_bf16(x):
    hi = x.astype(BF16)
    lo = (x - hi.astype(F32)).astype(BF16)
    return hi, lo


MAX_PREDS = 2


def _one_if(cond):
    return jnp.where(cond, 1.0, 0.0)


def _bit_search(count, nbits, start, ones, pred_of, accept):
    def step(i, t):
        cand = t | jnp.left_shift(ones, nbits - 1 - i)
        (n,) = count([pred_of(cand)])
        return jnp.where(accept(n), cand, t)

    return lax.fori_loop(0, nbits, step, start)


def _select_params(count, top, nbits_idx, ones):
    kf = float(top)
    zero = ones * 0
    (c0,) = count([lambda kk, idx, rs: _one_if(kk >= 0)])
    t0 = jnp.where(c0 >= kf, zero, zero + INT_MIN)
    t = _bit_search(count, 31, t0, ones, lambda cand: (lambda kk, idx, rs: _one_if(kk >= cand[rs])),
                    lambda n: n >= kf)
    cgt, cge = count([lambda kk, idx, rs: _one_if(kk > t[rs]), lambda kk, idx, rs: _one_if(kk >= t[rs])])
    r = kf - cgt

    def jsearch():
        return _bit_search(
            count, nbits_idx, zero, ones,
            lambda cand: (lambda kk, idx, rs: jnp.where(kk == t[rs], _one_if(idx < cand[rs]), 0.0)),
            lambda n: n < r)

    need = jnp.max(cge) > kf
    j = lax.cond(need, jsearch, lambda: zero + (2 ** nbits_idx))
    return t, j


def _bias_tile(kk, idx, t, j):
    tie = jnp.where(kk == t, jnp.where(idx <= j, 0.0, NEG_BIAS), NEG_BIAS)
    bias = jnp.where(kk > t, 0.0, tie)
    return jnp.where(kk == _NEG_INF_KEY, NEG_BIAS, bias)


def _const_spec(shape):
    nd = len(shape)
    return pl.BlockSpec(shape, lambda *_: (0,) * nd, pipeline_mode=pl.Buffered(1))


def _params(n_grid):
    return pltpu.CompilerParams(dimension_semantics=("arbitrary",) * n_grid, vmem_limit_bytes=VMEM_LIMIT)


def _ada_kernel(c_ref, w_ref, b_ref, o_ref):
    c = c_ref[...]
    a = c * jax.nn.sigmoid(c)
    o_ref[...] = jnp.dot(a, w_ref[...], preferred_element_type=F32,
                         precision=lax.Precision.HIGHEST) + b_ref[...]


def _ada(c_all, w_ada, b_ada):
    n, d = c_all.shape
    n6 = w_ada.shape[1]
    return pl.pallas_call(
        _ada_kernel,
        grid=(n6 // d,),
        in_specs=[pl.BlockSpec((n, d), lambda j: (0, 0)),
                  pl.BlockSpec((d, d), lambda j: (0, j)),
                  pl.BlockSpec((1, d), lambda j: (0, j))],
        out_specs=pl.BlockSpec((n, d), lambda j: (0, j)),
        out_shape=jax.ShapeDtypeStruct((n, n6), F32),
        compiler_params=_params(1),
        name="ada",
    )(c_all, w_ada, b_ada.reshape(1, n6))


def _modulated_norm(x, mod, g, d):
    ms = jnp.mean(x * x, axis=-1, keepdims=True)
    h = (x * lax.rsqrt(ms + EPS)) * g[None]
    return h * (1.0 + mod[:, :, d:2 * d]) + mod[:, :, 0:d]


def _rope_lanes(x2, nseq, rows, cos, sa, sb):
    m, w = x2.shape
    xm = pltpu.roll(x2, w - ROPE_HALF, 1)
    xp = pltpu.roll(x2, ROPE_HALF, 1)
    y = (x2.reshape(nseq, rows, w) * cos[None] + xm.reshape(nseq, rows, w) * sa[None]
         + xp.reshape(nseq, rows, w) * sb[None])
    return y.reshape(m, w)


def _rope_rows(x3, cos, sin):
    x1 = x3[:, 0:ROPE_HALF, :]
    x2 = x3[:, ROPE_HALF:ROPE_DIM, :]
    return jnp.concatenate([x1 * cos[None] - x2 * sin[None], x2 * cos[None] + x1 * sin[None],
                            x3[:, ROPE_DIM:, :]], axis=1)


def _qexp_output(q, nseq, rows, tabs, bd_ref, gq, qexp_ref):
    m, aw = q.shape
    nh = aw // HEAD_DIM
    cos, sa, sb = tabs
    reps = aw // LANES
    ss = jnp.dot((q * q).astype(BF16), bd_ref[...], preferred_element_type=F32)
    q = (q * lax.rsqrt(ss + EPS)) * gq
    q = _rope_lanes(q, nseq, rows, jnp.concatenate([cos] * reps, axis=1), jnp.concatenate([sa] * reps, axis=1),
                    jnp.concatenate([sb] * reps, axis=1))
    low = _lane_low((1, LANES))
    qs = q * (HEAD_DIM ** -0.5 * LOG2E)
    pieces = []
    for hh in range(nh):
        chunk = qs[:, (hh // 2) * LANES:(hh // 2 + 1) * LANES]
        keep = low if hh % 2 == 0 else jnp.logical_not(low)
        pieces.append(jnp.where(keep, chunk, 0.0))
    qexp_ref[...] = jnp.concatenate(pieces, axis=1).reshape(nseq, rows, nh * LANES).astype(qexp_ref.dtype)


def _q_outputs(q, pidx, nseq, rows, tabs, bd_ref, gq, qexp_ref, qi3_ref, wi_ref):
    m = q.shape[0]
    cos, sa, sb = tabs
    low = _lane_low((1, LANES))
    _qexp_output(q, nseq, rows, tabs, bd_ref, gq, qexp_ref)

    nqi = N_IDX_HEADS * IDX_DIM
    qi = _rope_lanes(pidx[:, 0:nqi], nseq, rows, jnp.concatenate([cos] * (nqi // LANES), axis=1),
                     jnp.concatenate([sa] * (nqi // LANES), axis=1), jnp.concatenate([sb] * (nqi // LANES), axis=1))
    wi_ref[...] = (pidx[:, nqi:nqi + LANES] * ((IDX_DIM ** -0.5) * (N_IDX_HEADS ** -0.5))).reshape(nseq, rows, LANES)
    zero = jnp.zeros((m, LANES), F32)
    q3 = []
    for t in range(nqi // LANES):
        xt = qi[:, t * LANES:(t + 1) * LANES]
        hi = xt.astype(BF16).astype(F32)
        lo = xt - hi
        hi_r = pltpu.roll(hi, LANES // 2, 1)
        lo_r = pltpu.roll(lo, LANES // 2, 1)
        q3 += [jnp.where(low, hi, hi_r), jnp.where(low, lo, zero),
               jnp.where(low, hi_r, hi), jnp.where(low, lo_r, zero)]
    qi3_ref[...] = jnp.concatenate(q3, axis=1).reshape(nseq, rows, N_IDX_HEADS * QI3).astype(qi3_ref.dtype)


def _conv_branch(pcv, nseq, rows, cw, wsc_ref, cbuf_ref, bconv_ref, cst_ref):
    bg = pcv[:, 0:cw].reshape(nseq, rows, cw)
    cu = (pcv[:, cw:2 * cw] * pcv[:, 2 * cw:3 * cw]).reshape(nseq, rows, cw)
    cbuf_ref[:, 8:, :] = cu
    wsc = wsc_ref[...]
    conv = wsc[0:1, :][None] * cbuf_ref[:, 6:6 + rows, :]
    conv = conv + wsc[1:2, :][None] * cbuf_ref[:, 7:7 + rows, :]
    conv = conv + wsc[2:3, :][None] * cu
    bconv_ref[...] = (bg * conv).astype(bconv_ref.dtype)
    cst_ref[...] = cu[:, rows - 2:rows, :]


def _inproj_p_kernel(x_ref, mod_ref, gmix_ref, wq_ref, wfm_ref, wfml_ref, wkih_ref, wkil_ref, wcv_ref, bd_ref,
                     gq_ref, gkc_ref, cos_ref, sa_ref, sb_ref, cost_ref, sint_ref, wsc_ref,
                     qexp_ref, qi3t_ref, wit_ref, ki3_ref, bconv_ref, cst_ref, kt_ref, vt_ref, kit_ref, ktb_ref,
                     vta_ref, cbuf_ref):
    _, rows, d = x_ref.shape
    aw = kt_ref.shape[1]
    cw = bconv_ref.shape[-1]
    nh = aw // HEAD_DIM
    nblk, _, blk = ktb_ref.shape[1:]
    nqi = N_IDX_HEADS * IDX_DIM

    h2 = _modulated_norm(x_ref[...], mod_ref[...], gmix_ref[...], d).reshape(rows, d)
    hb, hl = _split_bf16(h2)
    cos, sa, sb = cos_ref[...], sa_ref[...], sb_ref[...]
    cost = cost_ref[...]
    sint = sint_ref[...]

    _qexp_output(jnp.dot(hb, wq_ref[...], preferred_element_type=F32), 1, rows, (cos, sa, sb), bd_ref, gq_ref[...],
                 qexp_ref)

    fm = lax.dot_general(wfm_ref[...], hb, _NT, preferred_element_type=F32)
    o_i = 2 * aw
    idx = (fm[o_i:] + lax.dot_general(wfm_ref[o_i:, :], hl, _NT, preferred_element_type=F32)
           + lax.dot_general(wfml_ref[...], hb, _NT, preferred_element_type=F32))

    k3 = fm[0:aw].reshape(nh, HEAD_DIM, rows)
    ss = jnp.mean(k3 * k3, axis=1, keepdims=True)
    k3 = _rope_rows((k3 * lax.rsqrt(ss + EPS)) * gkc_ref[...][None], cost, sint)
    kt = k3.reshape(aw, rows)
    vt = fm[aw:2 * aw]
    qi = _rope_rows(idx[0:nqi].reshape(N_IDX_HEADS, IDX_DIM, rows), cost, sint)
    kit = _rope_rows(idx[nqi:nqi + IDX_DIM].reshape(1, IDX_DIM, rows), cost, sint).reshape(IDX_DIM, rows)
    kt_ref[0] = kt
    vt_ref[0] = vt
    kit_ref[0] = kit
    wit_ref[0] = idx[nqi + IDX_DIM:nqi + IDX_DIM + SUBLANES] * ((IDX_DIM ** -0.5) * (N_IDX_HEADS ** -0.5))

    zeros = jnp.zeros((HEAD_DIM, rows), BF16)
    ones = jnp.ones((HEAD_DIM, rows), BF16)
    qih = qi.astype(BF16)
    qil = (qi - qih.astype(F32)).astype(BF16)
    qi3t_ref[0] = jnp.concatenate(
        [b_ for hh in range(N_IDX_HEADS) for b_ in (qih[hh], qih[hh], qil[hh], zeros)], axis=0)
    ktb = kt.astype(BF16)
    vtb = vt.astype(BF16)
    vta = jnp.concatenate(
        [b_ for hh in range(nh) for b_ in
         ((vtb[hh * HEAD_DIM:(hh + 1) * HEAD_DIM], ones) if hh % 2 == 0 else
          (ones, vtb[hh * HEAD_DIM:(hh + 1) * HEAD_DIM]))], axis=0)
    for bb in range(nblk):
        cs = slice(bb * blk, (bb + 1) * blk)
        ktb_ref[0, bb] = ktb[:, cs]
        vta_ref[0, bb] = vta[:, cs]

    ki = (jnp.dot(hb, wkih_ref[...], preferred_element_type=F32)
          + jnp.dot(hl, wkih_ref[...], preferred_element_type=F32)
          + jnp.dot(hb, wkil_ref[...], preferred_element_type=F32))
    low = _lane_low((1, LANES))
    ki = _rope_lanes(ki, 1, rows, jnp.where(low, cos, 1.0), jnp.where(low, sa, 0.0), jnp.where(low, sb, 0.0))
    khi = ki.astype(BF16).astype(F32)
    klo = ki - khi
    zero = jnp.zeros((rows, LANES), F32)
    ki3_ref[0] = jnp.concatenate([jnp.where(low, khi, pltpu.roll(klo, LANES // 2, 1)), jnp.where(low, khi, zero)],
                                 axis=1).astype(BF16)

    i = pl.program_id(1)

    @pl.when(i == 0)
    def _():
        cbuf_ref[:, 6:8, :] = jnp.zeros((1, 2, cw), F32)

    @pl.when(i > 0)
    def _():
        cbuf_ref[:, 6:8, :] = cbuf_ref[:, rows + 6:rows + 8, :]

    _conv_branch(jnp.dot(hb, wcv_ref[...], preferred_element_type=F32), 1, rows, cw, wsc_ref, cbuf_ref,
                 bconv_ref, cst_ref)


def _inproj_prompt(x, mod, wts, tabs, tabs_t, *, rows, blk):
    nb, t, d = x.shape
    aw = d // 2
    cw = d - aw
    nh = aw // HEAD_DIM
    nblk = rows // blk
    tile = lambda c: pl.BlockSpec((1, rows, c), lambda b, i: (b, i, 0))
    per_seq = lambda r, c: pl.BlockSpec((1, r, c), lambda b, i: (b, 0, 0))
    ftile = lambda r: pl.BlockSpec((1, r, rows), lambda b, i: (b, 0, i))
    btile = lambda r: pl.BlockSpec((1, nblk, r, blk), lambda b, i: (b, i, 0, 0))
    tab = pl.BlockSpec((rows, LANES), lambda b, i: (i, 0))
    tab_t = pl.BlockSpec((ROPE_HALF, rows), lambda b, i: (0, i))
    cos, sa, sb = tabs
    cost, sint = tabs_t
    in_specs = [tile(d), per_seq(1, mod.shape[-1]), _const_spec((1, d)),
                _const_spec(wts["wq"].shape), _const_spec(wts["wfm"].shape), _const_spec(wts["wfml"].shape),
                _const_spec(wts["wkih"].shape), _const_spec(wts["wkils"].shape), _const_spec(wts["wcv"].shape),
                _const_spec(wts["bd"].shape), _const_spec((1, aw)), _const_spec((HEAD_DIM, 1)),
                tab, tab, tab, tab_t, tab_t, _const_spec((CONV_K, cw))]
    out_specs = [tile(nh * LANES), ftile(N_IDX_HEADS * QI3), ftile(SUBLANES), tile(QI3), tile(cw),
                 per_seq(CONV_K - 1, cw), ftile(aw), ftile(aw), ftile(IDX_DIM), btile(aw), btile(nh * LANES)]
    sds = jax.ShapeDtypeStruct
    out_shape = [sds((nb, t, nh * LANES), BF16), sds((nb, N_IDX_HEADS * QI3, t), BF16), sds((nb, SUBLANES, t), F32),
                 sds((nb, t, QI3), BF16), sds((nb, t, cw), BF16), sds((nb, CONV_K - 1, cw), F32),
                 sds((nb, aw, t), F32), sds((nb, aw, t), F32), sds((nb, IDX_DIM, t), F32),
                 sds((nb, t // blk, aw, blk), BF16), sds((nb, t // blk, nh * LANES, blk), BF16)]
    names = ["qexp", "qi3t", "wit", "ki3", "bconv", "cstate", "kt", "vt", "kit", "ktb", "vta"]
    res = pl.pallas_call(
        _inproj_p_kernel,
        grid=(nb, t // rows), in_specs=in_specs, out_specs=out_specs, out_shape=out_shape,
        scratch_shapes=[pltpu.VMEM((1, rows + 8, cw), F32)],
        compiler_params=_params(2),
        name="inproj_prompt",
    )(x, mod, wts["gmix"], wts["wq"], wts["wfm"], wts["wfml"], wts["wkih"], wts["wkils"], wts["wcv"], wts["bd"],
      wts["gq"], wts["gkc"], cos, sa, sb, cost, sint, wts["wsc"])
    return dict(zip(names, res))


def _inproj_s_kernel(x_ref, mod_ref, gmix_ref, wq_ref, wih_ref, wil_ref, wcv_ref, wkvs_ref, wkih_ref, wkil_ref,
                     bd_ref, gq_ref, gk_ref, cos_ref, sa_ref, sb_ref, wsc_ref, prev_ref,
                     qexp_ref, qi3_ref, wi_ref, bconv_ref, cst_ref, k_ref, v_ref, kidx_ref, ki3_ref, cbuf_ref):
    nseq, rows, d = x_ref.shape
    m = nseq * rows
    aw = k_ref.shape[-1]
    cw = bconv_ref.shape[-1]
    h2 = _modulated_norm(x_ref[...], mod_ref[...], gmix_ref[...], d).reshape(m, d)
    hb, hl = _split_bf16(h2)
    cos, sa, sb = cos_ref[...], sa_ref[...], sb_ref[...]

    q = jnp.dot(hb, wq_ref[...], preferred_element_type=F32)
    pidx = (jnp.dot(hb, wih_ref[...], preferred_element_type=F32)
            + jnp.dot(hl, wih_ref[...], preferred_element_type=F32)
            + jnp.dot(hb, wil_ref[...], preferred_element_type=F32))
    _q_outputs(q, pidx, nseq, rows, (cos, sa, sb), bd_ref, gq_ref[...], qexp_ref, qi3_ref, wi_ref)

    kv = jnp.dot(hb, wkvs_ref[...], preferred_element_type=F32)
    k = kv[:, 0:aw]
    ss = jnp.dot((k * k).astype(BF16), bd_ref[...], preferred_element_type=F32)
    k = (k * lax.rsqrt(ss + EPS)) * gk_ref[...]
    reps = aw // LANES
    k = _rope_lanes(k, nseq, rows, jnp.concatenate([cos] * reps, axis=1), jnp.concatenate([sa] * reps, axis=1),
                    jnp.concatenate([sb] * reps, axis=1))
    k_ref[...] = k.reshape(nseq, rows, aw)
    v_ref[...] = kv[:, aw:2 * aw].reshape(nseq, rows, aw)

    ki = (jnp.dot(hb, wkih_ref[...], preferred_element_type=F32)
          + jnp.dot(hl, wkih_ref[...], preferred_element_type=F32)
          + jnp.dot(hb, wkil_ref[...], preferred_element_type=F32))
    low = _lane_low((1, LANES))
    ki = _rope_lanes(ki, nseq, rows, jnp.where(low, cos, 1.0), jnp.where(low, sa, 0.0), jnp.where(low, sb, 0.0))
    kidx_ref[...] = ki[:, 0:IDX_DIM].reshape(nseq, rows, IDX_DIM)
    khi = ki.astype(BF16).astype(F32)
    klo = ki - khi
    zero = jnp.zeros((m, LANES), F32)
    k3 = jnp.concatenate([jnp.where(low, khi, pltpu.roll(klo, LANES // 2, 1)), jnp.where(low, khi, zero)], axis=1)
    ki3_ref[...] = k3.reshape(nseq, rows, QI3)

    cbuf_ref[:, 6:8, :] = prev_ref[...]
    _conv_branch(jnp.dot(hb, wcv_ref[...], preferred_element_type=F32), nseq, rows, cw, wsc_ref, cbuf_ref,
                 bconv_ref, cst_ref)


def _inproj_sample(x, mod, prev, wts, tabs, *, nseq):
    nb, rows, d = x.shape
    aw = d // 2
    cw = d - aw
    nh = aw // HEAD_DIM
    tile = lambda c: pl.BlockSpec((nseq, rows, c), lambda i: (i, 0, 0))
    per_seq = lambda r, c: pl.BlockSpec((nseq, r, c), lambda i: (i, 0, 0))
    tab = pl.BlockSpec((rows, LANES), lambda i: (0, 0))
    cos, sa, sb = tabs
    in_specs = [tile(d), per_seq(1, mod.shape[-1]), _const_spec((1, d)),
                _const_spec(wts["wq"].shape), _const_spec(wts["wih"].shape), _const_spec(wts["wil"].shape),
                _const_spec(wts["wcv"].shape), _const_spec(wts["wkvs"].shape), _const_spec(wts["wkih"].shape),
                _const_spec(wts["wkils"].shape), _const_spec(wts["bd"].shape), _const_spec((1, aw)),
                _const_spec((1, aw)), tab, tab, tab, _const_spec((CONV_K, cw)), per_seq(CONV_K - 1, cw)]
    cols = [("qexp", nh * LANES), ("qi3", N_IDX_HEADS * QI3), ("wi", LANES), ("bconv", cw)]
    cols2 = [("k", aw), ("v", aw), ("kidx", IDX_DIM), ("ki3", QI3)]
    out_specs = ([tile(c) for _, c in cols] + [per_seq(CONV_K - 1, cw)] + [tile(c) for _, c in cols2])
    sds = jax.ShapeDtypeStruct
    out_shape = ([sds((nb, rows, c), F32) for _, c in cols] + [sds((nb, CONV_K - 1, cw), F32)]
                 + [sds((nb, rows, c), F32) for _, c in cols2])
    names = [n for n, _ in cols] + ["cstate"] + [n for n, _ in cols2]
    res = pl.pallas_call(
        _inproj_s_kernel,
        grid=(nb // nseq,), in_specs=in_specs, out_specs=out_specs, out_shape=out_shape,
        scratch_shapes=[pltpu.VMEM((nseq, rows + 8, cw), F32)],
        compiler_params=_params(1),
        name="inproj_sample",
    )(x, mod, wts["gmix"], wts["wq"], wts["wih"], wts["wil"], wts["wcv"], wts["wkvs"], wts["wkih"], wts["wkils"],
      wts["bd"], wts["gq"], wts["gk"], cos, sa, sb, wts["wsc"], prev)
    return dict(zip(names, res))


def _pattn_kernel(qexp_ref, qi3t_ref, wit_ref, ki3_ref, ktb_ref, vta_ref, o_ref, keys_ref, m_ref, acc_ref,
                  *, top, blk, seq):
    nh = acc_ref.shape[0]
    nlt = blk // LANES
    nsl = blk // SUBLANES
    qb = pl.program_id(1)
    nkb = qb + 1
    q0 = qb * blk
    wit = wit_ref[0]
    rowk = lax.broadcasted_iota(I32, (blk, blk), 0)
    colq = lax.broadcasted_iota(I32, (blk, blk), 1)
    sub = lax.broadcasted_iota(I32, (SUBLANES, blk), 0)
    ones = jnp.ones((SUBLANES, blk), I32)
    every = slice(None)

    def score_body(kb, carry_):
        off = pl.multiple_of(kb * blk, blk)
        kblk = ki3_ref[0, pl.ds(off, blk), :]
        acc = jnp.zeros((blk, blk), F32)
        for hh in range(N_IDX_HEADS):
            dd = jnp.dot(kblk, qi3t_ref[0, hh * QI3:(hh + 1) * QI3, :], preferred_element_type=F32)
            acc = acc + jnp.maximum(dd, 0.0) * wit[hh:hh + 1, :]
        acc = jnp.where(acc == 0.0, 0.0, acc)
        acc = jnp.where(kb * blk + rowk <= q0 + colq, acc, -jnp.inf)
        keys_ref[kb] = _sortable(acc)
        return carry_

    lax.fori_loop(0, nkb, score_body, 0)

    nchain = 4

    def count(preds):
        def body(kb, cs):
            cs = [list(c) for c in cs]
            for g in range(nsl):
                kk = keys_ref[kb, g * SUBLANES:(g + 1) * SUBLANES, :]
                idx = kb * blk + g * SUBLANES + sub
                for a, pred in enumerate(preds):
                    cs[a][g % nchain] = cs[a][g % nchain] + pred(kk, idx, every)
            return tuple(tuple(c) for c in cs)

        zero_c = tuple(jnp.zeros((SUBLANES, blk), F32) for _ in range(nchain))
        cs = lax.fori_loop(0, nkb, body, tuple(zero_c for _ in preds))
        return [jnp.sum(sum(c[1:], c[0]), axis=0, keepdims=True) for c in cs]

    t, j = _select_params(count, top, (seq - 1).bit_length(), ones)

    m_ref[...] = jnp.full(m_ref.shape, -jnp.inf, F32)
    acc_ref[...] = jnp.zeros(acc_ref.shape, F32)
    t_full = jnp.concatenate([t] * nsl, axis=0)
    j_full = jnp.concatenate([j] * nsl, axis=0)

    def attn_body(kb, carry_):
        bias = _bias_tile(keys_ref[kb], kb * blk + rowk, t_full, j_full).T
        for hh in range(nh):
            pp = hh // 2
            qh = qexp_ref[0, :, hh * LANES:(hh + 1) * LANES]
            kh = ktb_ref[0, kb, pp * LANES:(pp + 1) * LANES, :]
            va = vta_ref[0, kb, hh * LANES:(hh + 1) * LANES, :]
            s = jnp.dot(qh, kh, preferred_element_type=F32) + bias
            m_prev = m_ref[hh]
            m_new = jnp.maximum(m_prev, jnp.max(s, axis=1, keepdims=True))
            alpha = jnp.exp2(m_prev - m_new)
            p = jnp.exp2(s - jnp.concatenate([m_new] * nlt, axis=1))
            acc_ref[hh] = alpha * acc_ref[hh] + lax.dot_general(p.astype(BF16), va, _NT, preferred_element_type=F32)
            m_ref[hh] = m_new
        return carry_

    lax.fori_loop(0, nkb, attn_body, 0)

    low = _lane_low((1, LANES))
    outs = []
    for pp in range(nh // 2):
        a0 = acc_ref[2 * pp]
        a1 = acc_ref[2 * pp + 1]
        outs.append(jnp.where(low, a0 / pltpu.roll(a0, LANES // 2, 1), a1 / pltpu.roll(a1, LANES // 2, 1)))
    o_ref[0] = jnp.concatenate(outs, axis=1).astype(o_ref.dtype)


def _pattn(p, *, top, blk):
    nb, nblk, aw, _ = p["ktb"].shape
    seq = nblk * blk
    nh = aw // HEAD_DIM
    qtile = lambda c: pl.BlockSpec((1, blk, c), lambda b, i: (b, i, 0))
    ftile = lambda r: pl.BlockSpec((1, r, blk), lambda b, i: (b, 0, i))
    blocks = lambda r: pl.BlockSpec((1, nblk, r, blk), lambda b, i: (b, 0, 0, 0))
    return pl.pallas_call(
        functools.partial(_pattn_kernel, top=top, blk=blk, seq=seq),
        grid=(nb, nblk),
        in_specs=[qtile(nh * LANES), ftile(N_IDX_HEADS * QI3), ftile(SUBLANES),
                  pl.BlockSpec((1, seq, QI3), lambda b, i: (b, 0, 0)), blocks(aw), blocks(nh * LANES)],
        out_specs=qtile(aw),
        out_shape=jax.ShapeDtypeStruct((nb, seq, aw), BF16),
        scratch_shapes=[pltpu.VMEM((nblk, blk, blk), I32),
                        pltpu.VMEM((nh, blk, LANES), F32),
                        pltpu.VMEM((nh, blk, LANES), F32)],
        compiler_params=_params(2),
        name="pattn",
    )(p["qexp"], p["qi3t"], p["wit"], p["ki3"], p["ktb"], p["vta"])


def _sidx_kernel(pt_ref, qi3_ref, wi_ref, kin_ref, kidx_hbm, keys_ref, buf_ref, sem_ref, *, npages):
    t_new = qi3_ref.shape[1]
    b = pl.program_id(0)
    nb = pl.num_programs(0)
    slot = lax.rem(b, 2)

    def page_copy(seq_i, i, sl):
        return pltpu.make_async_copy(kidx_hbm.at[pt_ref[seq_i, i]], buf_ref.at[sl, i], sem_ref.at[sl])

    @pl.when(b == 0)
    def _():
        for i in range(npages):
            page_copy(0, i, 0).start()

    @pl.when(b + 1 < nb)
    def _():
        for i in range(npages):
            page_copy(b + 1, i, 1 - slot).start()

    qi3 = qi3_ref[0]
    wi = wi_ref[0]
    heads = range(N_IDX_HEADS)
    lhs = jnp.concatenate([qi3[:, hh * QI3:hh * QI3 + LANES] for hh in heads]
                          + [qi3[:, hh * QI3 + LANES:(hh + 1) * QI3] for hh in heads], axis=0).astype(BF16)

    def head_sum(dd):
        acc = jnp.zeros((t_new, LANES), F32)
        for hh in heads:
            acc = acc + jnp.maximum(dd[hh * t_new:(hh + 1) * t_new], 0.0) * wi[:, hh:hh + 1]
        return jnp.where(acc == 0.0, 0.0, acc)

    for i in range(npages):
        page_copy(b, i, slot).wait()

    group = math.gcd(npages, 8)

    def page_body(g, carry_):
        for u in range(group):
            i = g * group + u
            kp = buf_ref[slot, i]
            kh, kl = _split_bf16(kp)
            d2 = jnp.dot(lhs, jnp.concatenate([kh, kl], axis=0), preferred_element_type=F32)
            nq = N_IDX_HEADS * t_new
            keys_ref[0, i] = _sortable(head_sum(d2[0:nq] + d2[nq:2 * nq]))
        return carry_

    lax.fori_loop(0, npages // group, page_body, 0)

    q3 = jnp.concatenate([qi3[:, hh * QI3:(hh + 1) * QI3] for hh in heads], axis=0).astype(BF16)
    kn = jnp.concatenate([kin_ref[0], jnp.zeros((LANES - t_new, QI3), F32)], axis=0).astype(BF16)
    sc = head_sum(lax.dot_general(q3, kn, _NT, preferred_element_type=F32))
    rown = lax.broadcasted_iota(I32, (t_new, LANES), 0)
    lanen = lax.broadcasted_iota(I32, (t_new, LANES), 1)
    keys_ref[0, npages] = _sortable(jnp.where(lanen <= rown, sc, -jnp.inf))


def _sidx(s, kidx_t, page_table):
    nb, t_new, _ = s["qi3"].shape
    npages = page_table.shape[1]
    seq_blk = lambda c: pl.BlockSpec((1, t_new, c), lambda b, pt: (b, 0, 0))
    grid_spec = pltpu.PrefetchScalarGridSpec(
        num_scalar_prefetch=1, grid=(nb,),
        in_specs=[seq_blk(N_IDX_HEADS * QI3), seq_blk(LANES), seq_blk(QI3), pl.BlockSpec(memory_space=pl.ANY)],
        out_specs=pl.BlockSpec((1, npages + 1, t_new, LANES), lambda b, pt: (b, 0, 0, 0)),
        scratch_shapes=[pltpu.VMEM((2, npages, IDX_DIM, PAGE_SIZE), F32), pltpu.SemaphoreType.DMA((2,))])
    return pl.pallas_call(
        functools.partial(_sidx_kernel, npages=npages),
        grid_spec=grid_spec,
        out_shape=jax.ShapeDtypeStruct((nb, npages + 1, t_new, LANES), I32),
        compiler_params=_params(1),
        name="sidx",
    )(page_table, s["qi3"], s["wi"], s["ki3"], kidx_t)


def _ssel_kernel(keys_ref, bias_ref, cnt_ref, *, top):
    nseq, np1, t_new, _ = keys_ref.shape
    rows = nseq * t_new
    lane = lax.broadcasted_iota(I32, (rows, LANES), 1)
    ones = jnp.ones((rows, LANES), I32)
    cseq = math.gcd(nseq, 16)
    crow = cseq * t_new
    lane_c = lax.broadcasted_iota(I32, (crow, LANES), 1)

    def count(preds):
        na = len(preds)
        cnt_ref[0:na] = jnp.zeros((na,) + cnt_ref.shape[1:], F32)

        def body(pg, carry_):
            idx = pg * LANES + lane_c
            for ch in range(nseq // cseq):
                rs = slice(ch * crow, (ch + 1) * crow)
                kk = keys_ref[ch * cseq:(ch + 1) * cseq, pg].reshape(crow, LANES)
                for a in range(na):
                    cnt_ref[a, rs, :] = cnt_ref[a, rs, :] + preds[a](kk, idx, rs)
            return carry_

        lax.fori_loop(0, np1, body, 0)
        return [jnp.sum(cnt_ref[a], axis=1, keepdims=True) for a in range(na)]

    t, j = _select_params(count, top, (np1 * LANES - 1).bit_length(), ones)

    def bias_body(pg, carry_):
        bias_ref[:, pg] = _bias_tile(keys_ref[:, pg].reshape(rows, LANES), pg * LANES + lane, t, j).reshape(
            nseq, t_new, LANES)
        return carry_

    lax.fori_loop(0, np1, bias_body, 0)


def _ssel(keys, *, top, nseq):
    nb, np1, t_new, _ = keys.shape
    blk_spec = pl.BlockSpec((nseq, np1, t_new, LANES), lambda i: (i, 0, 0, 0))
    return pl.pallas_call(
        functools.partial(_ssel_kernel, top=top),
        grid=(nb // nseq,), in_specs=[blk_spec], out_specs=blk_spec,
        out_shape=jax.ShapeDtypeStruct(keys.shape, F32),
        scratch_shapes=[pltpu.VMEM((MAX_PREDS, nseq * t_new, LANES), F32)],
        compiler_params=_params(1),
        name="ssel",
    )(keys)


def _sattn_kernel(pt_ref, qexp_ref, bias_ref, kn_ref, vn_ref, k_hbm, v_hbm, o_ref,
                  kbuf_ref, vbuf_ref, sem_ref, qbd_ref, m_ref, l_ref, acc_ref, *, cpp, nchunks):
    spb, t_new, aw = kn_ref.shape
    nh = aw // HEAD_DIM
    npair = nh // 2
    b = pl.program_id(0)
    c = pl.program_id(1)
    step = b * nchunks + c
    total = pl.num_programs(0) * nchunks
    slot = lax.rem(step, 2)

    def copies(grp, ch, sl):
        out = []
        for q_ in range(spb):
            for i in range(cpp):
                pg = pt_ref[grp * spb + q_, ch * cpp + i]
                out.append(pltpu.make_async_copy(k_hbm.at[pg], kbuf_ref.at[sl, q_, i], sem_ref.at[0, sl]))
                out.append(pltpu.make_async_copy(v_hbm.at[pg], vbuf_ref.at[sl, q_, i], sem_ref.at[1, sl]))
        return out

    @pl.when(step == 0)
    def _():
        for cp in copies(0, 0, 0):
            cp.start()

    @pl.when(step + 1 < total)
    def _():
        nxt = step + 1
        for cp in copies(nxt // nchunks, lax.rem(nxt, nchunks), 1 - slot):
            cp.start()

    @pl.when(c == 0)
    def _():
        zero = jnp.zeros((t_new, LANES), F32)
        for q_ in range(spb):
            qe = qexp_ref[q_]
            rows_ = []
            for hh in range(nh):
                chunk = qe[:, hh * LANES:(hh + 1) * LANES]
                rows_.append(jnp.concatenate([chunk if tt == hh // 2 else zero for tt in range(npair)], axis=1))
            qbd_ref[q_] = jnp.concatenate(rows_, axis=0).astype(BF16)
        m_ref[...] = jnp.full(m_ref.shape, -jnp.inf, F32)
        l_ref[...] = jnp.zeros(l_ref.shape, F32)
        acc_ref[...] = jnp.zeros(acc_ref.shape, F32)

    def update(q_, s, pv_fn):
        m_prev = m_ref[q_]
        m_new = jnp.maximum(m_prev, jnp.max(s, axis=1, keepdims=True))
        alpha = jnp.exp2(m_prev - m_new)
        p = jnp.exp2(s - jnp.concatenate([m_new] * (s.shape[1] // LANES), axis=1))
        l_ref[q_] = alpha * l_ref[q_] + jnp.sum(p, axis=1, keepdims=True)
        acc_ref[q_] = jnp.concatenate([alpha] * (aw // LANES), axis=1) * acc_ref[q_] + pv_fn(p.astype(BF16))
        m_ref[q_] = m_new

    for cp in copies(b, c, slot):
        cp.wait()

    for q_ in range(spb):
        kall = jnp.concatenate([kbuf_ref[slot, q_, i] for i in range(cpp)], axis=1).astype(BF16)
        vall = jnp.concatenate([vbuf_ref[slot, q_, i] for i in range(cpp)], axis=1).astype(BF16)
        ball = jnp.concatenate([bias_ref[q_, c * cpp + i] for i in range(cpp)], axis=1)
        s_all = jnp.dot(qbd_ref[q_], kall, preferred_element_type=F32) + jnp.concatenate([ball] * nh, axis=0)
        update(q_, s_all, lambda pb, vall=vall: lax.dot_general(pb, vall, _NT, preferred_element_type=F32))

    @pl.when(c == nchunks - 1)
    def _():
        pad = jnp.zeros((LANES - t_new, aw), F32)
        low = _lane_low((1, LANES))
        for q_ in range(spb):
            knp = jnp.concatenate([kn_ref[q_], pad], axis=0).astype(BF16)
            vnp = jnp.concatenate([vn_ref[q_], pad], axis=0).astype(BF16)
            s = (lax.dot_general(qbd_ref[q_], knp, _NT, preferred_element_type=F32)
                 + jnp.concatenate([bias_ref[q_, nchunks * cpp]] * nh, axis=0))
            update(q_, s, lambda pb, vnp=vnp: jnp.dot(pb, vnp, preferred_element_type=F32))
            acc = acc_ref[q_]
            inv = 1.0 / l_ref[q_]
            outs = []
            for pp in range(npair):
                r0 = slice(2 * pp * t_new, (2 * pp + 1) * t_new)
                r1 = slice((2 * pp + 1) * t_new, (2 * pp + 2) * t_new)
                cs = slice(pp * LANES, (pp + 1) * LANES)
                outs.append(jnp.where(low, acc[r0, cs] * inv[r0], acc[r1, cs] * inv[r1]))
            o_ref[q_] = jnp.concatenate(outs, axis=1)


def _sattn(s, bias, k_t, v_t, page_table, *, cpp, spb):
    nb, t_new, aw = s["k"].shape
    nh = aw // HEAD_DIM
    npages = page_table.shape[1]
    nchunks = npages // cpp
    seq_blk = lambda c: pl.BlockSpec((spb, t_new, c), lambda b, ch, pt: (b, 0, 0))
    grid_spec = pltpu.PrefetchScalarGridSpec(
        num_scalar_prefetch=1, grid=(nb // spb, nchunks),
        in_specs=[seq_blk(nh * LANES),
                  pl.BlockSpec((spb, npages + 1, t_new, LANES), lambda b, ch, pt: (b, 0, 0, 0)),
                  seq_blk(aw), seq_blk(aw),
                  pl.BlockSpec(memory_space=pl.ANY), pl.BlockSpec(memory_space=pl.ANY)],
        out_specs=seq_blk(aw),
        scratch_shapes=[pltpu.VMEM((2, spb, cpp, aw, PAGE_SIZE), F32),
                        pltpu.VMEM((2, spb, cpp, aw, PAGE_SIZE), F32),
                        pltpu.SemaphoreType.DMA((2, 2)),
                        pltpu.VMEM((spb, nh * t_new, aw), BF16),
                        pltpu.VMEM((spb, nh * t_new, LANES), F32),
                        pltpu.VMEM((spb, nh * t_new, LANES), F32),
                        pltpu.VMEM((spb, nh * t_new, aw), F32)])
    return pl.pallas_call(
        functools.partial(_sattn_kernel, cpp=cpp, nchunks=nchunks),
        grid_spec=grid_spec,
        out_shape=jax.ShapeDtypeStruct((nb, t_new, aw), F32),
        compiler_params=_params(2),
        name="sattn",
    )(page_table, s["qexp"], bias, s["k"], s["v"], k_t, v_t)


def _ffn_kernel(x_ref, mod_ref, attn_ref, bconv_ref, wot_ref, wob_ref, g2_ref, wg_ref, wu_ref, wfc_ref, bfc_ref,
                wd_ref, prev_ref, y_ref, fst_ref, gbuf_ref, *, carry, fchunk):
    nseq, rows, d = x_ref.shape
    m = nseq * rows
    f = wg_ref.shape[1]
    x = x_ref[...]
    mod = mod_ref[...]
    gt1 = mod[:, :, 2 * d:3 * d]
    gt2 = mod[:, :, 5 * d:6 * d]
    a = attn_ref[...].reshape(m, attn_ref.shape[-1]).astype(BF16)
    bc = bconv_ref[...].reshape(m, bconv_ref.shape[-1]).astype(BF16)
    mix = (jnp.dot(a, wot_ref[...], preferred_element_type=F32)
           + jnp.dot(bc, wob_ref[...], preferred_element_type=F32))
    x1 = x + gt1 * mix.reshape(nseq, rows, d)
    hb = _modulated_norm(x1, mod[:, :, 3 * d:5 * d], g2_ref[...], d).reshape(m, d).astype(BF16)

    if carry:
        i = pl.program_id(1)

        @pl.when(i == 0)
        def _():
            gbuf_ref[:, 6:8, :] = jnp.zeros((nseq, 2, f), F32)

        @pl.when(i > 0)
        def _():
            gbuf_ref[:, 6:8, :] = gbuf_ref[:, rows + 6:rows + 8, :]
    else:
        gbuf_ref[:, 6:8, :] = prev_ref[...]

    acc = jnp.zeros((m, d), F32)
    for fc in range(f // fchunk):
        sl = slice(fc * fchunk, (fc + 1) * fchunk)
        g = jnp.dot(hb, wg_ref[:, sl], preferred_element_type=F32).reshape(nseq, rows, fchunk)
        u = jnp.dot(hb, wu_ref[:, sl], preferred_element_type=F32).reshape(nseq, rows, fchunk)
        gbuf_ref[:, 8:, sl] = g
        wfc = wfc_ref[:, sl]
        gc = wfc[0:1, :][None] * gbuf_ref[:, 6:6 + rows, sl]
        gc = gc + wfc[1:2, :][None] * gbuf_ref[:, 7:7 + rows, sl]
        gc = gc + wfc[2:3, :][None] * g
        z = gc + bfc_ref[:, sl][None]
        act = (z * jax.nn.sigmoid(z)) * u
        acc = acc + jnp.dot(act.reshape(m, fchunk).astype(BF16), wd_ref[sl, :], preferred_element_type=F32)
        fst_ref[:, :, sl] = g[:, rows - 2:rows, :]
    y_ref[...] = x1 + gt2 * acc.reshape(nseq, rows, d)


def _ffn(x, mod, attn, bconv, prev, wts, *, carry, rows, nseq, fchunk):
    nb, t, d = x.shape
    f = wts["wg"].shape[1]
    if carry:
        grid = (nb, t // rows)
        tile = lambda c: pl.BlockSpec((1, rows, c), lambda b, i: (b, i, 0))
        per_seq = lambda r, c: pl.BlockSpec((1, r, c), lambda b, i: (b, 0, 0))
    else:
        grid = (1, nb // nseq)
        tile = lambda c: pl.BlockSpec((nseq, rows, c), lambda b, i: (i, 0, 0))
        per_seq = lambda r, c: pl.BlockSpec((nseq, r, c), lambda b, i: (i, 0, 0))
    in_specs = [tile(d), per_seq(1, mod.shape[-1]), tile(attn.shape[-1]), tile(bconv.shape[-1]),
                _const_spec(wts["wot"].shape), _const_spec(wts["wob"].shape), _const_spec((1, d)),
                _const_spec(wts["wg"].shape), _const_spec(wts["wu"].shape), _const_spec((CONV_K, f)),
                _const_spec((1, f)), _const_spec(wts["wd"].shape), per_seq(CONV_K - 1, f)]
    return pl.pallas_call(
        functools.partial(_ffn_kernel, carry=carry, fchunk=fchunk),
        grid=grid, in_specs=in_specs,
        out_specs=[tile(d), per_seq(CONV_K - 1, f)],
        out_shape=[jax.ShapeDtypeStruct((nb, t, d), F32), jax.ShapeDtypeStruct((nb, CONV_K - 1, f), F32)],
        scratch_shapes=[pltpu.VMEM((nseq, rows + 8, f), F32)],
        compiler_params=_params(2),
        name="ffn_prompt" if carry else "ffn_sample",
    )(x, mod, attn, bconv, wts["wot"], wts["wob"], wts["g2"], wts["wg"], wts["wu"], wts["wfc"], wts["bfc"],
      wts["wd"], prev)


def _rope_angles(pos):
    inv = ROPE_THETA ** (-jnp.arange(0, ROPE_DIM, 2, dtype=F32) / ROPE_DIM)
    ang = pos.astype(F32)[:, None] * inv[None, :]
    return jnp.cos(ang), jnp.sin(ang)


def _rope_tables(pos):
    cos, sin = _rope_angles(pos)
    t = pos.shape[0]
    pad1 = jnp.ones((t, HEAD_DIM - ROPE_DIM), F32)
    pad0 = jnp.zeros((t, HEAD_DIM - ROPE_DIM), F32)
    zh = jnp.zeros((t, ROPE_HALF), F32)
    c64 = jnp.concatenate([cos, cos, pad1], axis=1)
    a64 = jnp.concatenate([-sin, zh, pad0], axis=1)
    b64 = jnp.concatenate([zh, sin, pad0], axis=1)
    rep = LANES // HEAD_DIM
    return tuple(jnp.tile(x, (1, rep)) for x in (c64, a64, b64))


def _layer_weights(l, d, g_mix_norm, w_in, g_q, g_k, w_short_conv, w_o, g_ffn_norm, w_gate, w_up, w_ffn_conv,
                   b_ffn_conv, w_down):
    aw = d // 2
    cw = d - aw
    nh = aw // HEAD_DIM
    w = w_in[l]
    nqi = N_IDX_HEADS * IDX_DIM
    o_qi = 3 * aw
    o_ki = o_qi + nqi
    o_wi = o_ki + IDX_DIM
    o_cv = o_wi + N_IDX_HEADS
    widx = jnp.concatenate([w[:, o_qi:o_ki], w[:, o_wi:o_cv],
                            jnp.zeros((d, IDX_PAD - nqi - N_IDX_HEADS), F32)], axis=1)
    wih, wil = _split_bf16(widx)
    wki = w[:, o_ki:o_wi]
    wkih, wkil = _split_bf16(wki)
    pad_ki = lambda a: jnp.pad(a, ((0, 0), (0, LANES - IDX_DIM)))
    head = jnp.arange(aw) // HEAD_DIM
    bd = jnp.where(head[:, None] == head[None, :], 1.0 / HEAD_DIM, 0.0).astype(BF16)
    w_fm = jnp.concatenate([w[:, aw:o_cv], jnp.zeros((d, 2 * SUBLANES - N_IDX_HEADS), F32)], axis=1)
    wfm, wfml = _split_bf16(w_fm)
    return dict(
        gmix=g_mix_norm[l].reshape(1, d),
        wq=w[:, 0:aw].astype(BF16), wih=wih, wil=wil,
        wcv=w[:, o_cv:o_cv + 3 * cw].astype(BF16), bd=bd,
        wfm=wfm.T, wfml=wfml[:, o_qi - aw:].T,
        wkvs=w[:, aw:3 * aw].astype(BF16), wkih=pad_ki(wkih), wkils=pad_ki(wkil),
        gq=jnp.tile(g_q[l], nh).reshape(1, aw), gk=jnp.tile(g_k[l], nh).reshape(1, aw),
        gkc=g_k[l].reshape(HEAD_DIM, 1),
        wsc=w_short_conv[l],
        wot=w_o[l][0:aw].astype(BF16), wob=w_o[l][aw:].astype(BF16),
        g2=g_ffn_norm[l].reshape(1, d),
        wg=w_gate[l].astype(BF16), wu=w_up[l].astype(BF16), wfc=w_ffn_conv[l],
        bfc=b_ffn_conv[l].reshape(1, -1), wd=w_down[l].astype(BF16))


def _pick(n, candidates):
    for c in candidates:
        if n % c == 0:
            return c
    raise ValueError(f"no tile size in {candidates} divides {n}")


def kernel(x_prompt, x_sample, cache_k, cache_v, cache_kidx, state_conv, state_ffn_conv, page_table, c_prompt,
           c_sample, w_ada, b_ada, g_mix_norm, w_in, g_q, g_k, w_short_conv, w_o, g_ffn_norm, w_gate, w_up,
           w_ffn_conv, b_ffn_conv, w_down):
    nbp, seq, d = x_prompt.shape
    nbs, t_new, _ = x_sample.shape
    depth = w_in.shape[0]
    aw = d // 2
    cw = d - aw
    nh = aw // HEAD_DIM
    f = w_gate.shape[-1]
    npages = page_table.shape[1]
    past = npages * PAGE_SIZE
    n_pool = cache_k.shape[1]
    assert t_new == SUBLANES and cache_k.shape[2] == PAGE_SIZE and aw % LANES == 0 and nh % 2 == 0

    blk = _pick(seq, (256, 128))
    rows_in = _pick(seq, (512, 256, 128))
    rows_ffn = _pick(seq, (256, 128))
    nseq = _pick(nbs, (32, 16, 8, 4, 2, 1))
    cpp = _pick(npages, (8, 4, 2))
    fchunk = f // 2 if (f // 2) % LANES == 0 else f
    top_p = min(TOPK_MAX, seq // TOPK_FRAC)
    top_s = min(TOPK_MAX, (past + t_new) // TOPK_FRAC)

    pos_p = jnp.arange(seq)
    tabs_p = _rope_tables(pos_p)
    tabs_pt = tuple(a.T for a in _rope_angles(pos_p))
    tabs_s = _rope_tables(past + jnp.arange(t_new))
    zeros_ffn = jnp.zeros((nbp, CONV_K - 1, f), F32)

    xp, xs = x_prompt, x_sample
    outs = {k: [] for k in ("kp", "vp", "kip", "cp", "fp", "ks", "vs", "kis", "cs", "fs")}
    for l in range(depth):
        wts = _layer_weights(l, d, g_mix_norm, w_in, g_q, g_k, w_short_conv, w_o, g_ffn_norm, w_gate, w_up,
                             w_ffn_conv, b_ffn_conv, w_down)
        mod = _ada(jnp.concatenate([c_prompt, c_sample], axis=0), w_ada[l], b_ada[l])
        mod_p = mod[:nbp].reshape(nbp, 1, 6 * d)
        mod_s = mod[nbp:].reshape(nbs, 1, 6 * d)

        p = _inproj_prompt(xp, mod_p, wts, tabs_p, tabs_pt, rows=rows_in, blk=blk)
        attn_p = _pattn(p, top=top_p, blk=blk)
        xp, fst_p = _ffn(xp, mod_p, attn_p, p["bconv"], zeros_ffn, wts, carry=True, rows=rows_ffn, nseq=1,
                         fchunk=fchunk)

        k_t = cache_k[l].transpose(0, 2, 3, 1).reshape(n_pool, aw, PAGE_SIZE)
        v_t = cache_v[l].transpose(0, 2, 3, 1).reshape(n_pool, aw, PAGE_SIZE)
        kidx_t = cache_kidx[l].transpose(0, 2, 1)
        s = _inproj_sample(xs, mod_s, state_conv[l], wts, tabs_s, nseq=nseq)
        keys = _sidx(s, kidx_t, page_table)
        bias = _ssel(keys, top=top_s, nseq=nseq)
        attn_s = _sattn(s, bias, k_t, v_t, page_table, cpp=cpp, spb=_pick(nbs, (4, 2, 1)))
        xs, fst_s = _ffn(xs, mod_s, attn_s, s["bconv"], state_ffn_conv[l], wts, carry=False, rows=t_new, nseq=nseq,
                         fchunk=fchunk)

        outs["kp"].append(p["kt"].reshape(nbp, nh, HEAD_DIM, seq).transpose(0, 3, 1, 2))
        outs["vp"].append(p["vt"].reshape(nbp, nh, HEAD_DIM, seq).transpose(0, 3, 1, 2))
        outs["kip"].append(p["kit"].transpose(0, 2, 1))
        outs["cp"].append(p["cstate"])
        outs["fp"].append(fst_p)
        outs["ks"].append(s["k"].reshape(nbs, t_new, nh, HEAD_DIM))
        outs["vs"].append(s["v"].reshape(nbs, t_new, nh, HEAD_DIM))
        outs["kis"].append(s["kidx"])
        outs["cs"].append(s["cstate"])
        outs["fs"].append(fst_s)

    st = lambda k: jnp.stack(outs[k])
    return (xp, xs, st("kp"), st("vp"), st("kip"), st("cp"), st("fp"),
            st("ks"), st("vs"), st("kis"), st("cs"), st("fs"))
```
